```python
import jax, jax.numpy as jnp
from jax import lax
import numpy as np

D_MODEL = 1024
BATCH = 4
SEQ = 4096
DEPTH = 2
DEC_BATCH = 32
DEC_SEQ = 1
PAST_LEN = 16384
PAGE_SIZE = 128

HEAD_DIM = 64
NSA_HEADS = 8
NSA_KV_HEADS = 2
NSA_GROUP = NSA_HEADS // NSA_KV_HEADS
FOX_HEADS = 8
CMP_BLOCK = 64
N_SELECT = 16
WINDOW = 512
Q_BLOCK = 128
PLE_DIM = 256
D_FF = -(-8 * D_MODEL // (3 * 256)) * 256
ROPE_THETA = 10000.0
EPS = 1e-6
NEG_INF = -1e30
FORCED = 1e4
IN_WIDTH = (NSA_HEADS * HEAD_DIM + 6 * NSA_KV_HEADS * HEAD_DIM + 3 * NSA_HEADS
            + 3 * FOX_HEADS * HEAD_DIM + FOX_HEADS + 2 * D_MODEL)

kernel_name = "nsa_fox_gated_hybrid_step"


def rms_norm(x, g):
    xf = x.astype(jnp.float32)
    y = xf * lax.rsqrt(jnp.mean(xf * xf, axis=-1, keepdims=True) + EPS)
    return (y * g.astype(jnp.float32)).astype(x.dtype)


def rope(x, pos):
    half = HEAD_DIM // 2
    inv_freq = ROPE_THETA ** (-jnp.arange(half, dtype=jnp.float32) / half)
    ang = pos.astype(jnp.float32)[:, None] * inv_freq[None, :]
    cos = jnp.cos(ang)[:, None, :]
    sin = jnp.sin(ang)[:, None, :]
    xf = x.astype(jnp.float32)
    x1, x2 = xf[..., :half], xf[..., half:]
    return jnp.concatenate([x1 * cos - x2 * sin, x2 * cos + x1 * sin], axis=-1).astype(x.dtype)


def masked_softmax(s, mask):
    s = jnp.where(mask, s, NEG_INF)
    p = jax.nn.softmax(s, axis=-1)
    return jnp.where(mask, p, 0.0)


def mixer_inputs(h, pos, w_in, q_norm_nsa, k_norm_nsa, q_norm_fox, k_norm_fox, fox_fbias):
    B, T, _ = h.shape
    z = jnp.einsum('btd,de->bte', h, w_in)
    nkv = NSA_KV_HEADS * HEAD_DIM
    nfx = FOX_HEADS * HEAD_DIM
    sizes = [NSA_HEADS * HEAD_DIM, nkv, nkv, nkv, nkv, nkv, nkv, 3 * NSA_HEADS,
             nfx, nfx, nfx, FOX_HEADS, 2 * D_MODEL]
    q_a, kc, vc, ks, vs, kw, vw, g_a, q_b, k_b, v_b, f_b, g_m = jnp.split(
        z, np.cumsum(sizes)[:-1].tolist(), axis=-1)
    heads = lambda t, n: t.reshape(B, T, n, HEAD_DIM)
    q_a = rope(rms_norm(heads(q_a, NSA_HEADS), q_norm_nsa), pos)
    kc = rope(rms_norm(heads(kc, NSA_KV_HEADS), k_norm_nsa[0]), pos)
    ks = rope(rms_norm(heads(ks, NSA_KV_HEADS), k_norm_nsa[1]), pos)
    kw = rope(rms_norm(heads(kw, NSA_KV_HEADS), k_norm_nsa[2]), pos)
    nsa_rows = jnp.stack([kc, heads(vc, NSA_KV_HEADS), ks, heads(vs, NSA_KV_HEADS)], axis=2)
    win_rows = jnp.stack([kw, heads(vw, NSA_KV_HEADS)], axis=2)
    g_a = jax.nn.sigmoid(g_a).reshape(B, T, NSA_HEADS, 3)
    q_b = rms_norm(heads(q_b, FOX_HEADS), q_norm_fox)
    k_b = rms_norm(heads(k_b, FOX_HEADS), k_norm_fox)
    fox_rows = jnp.stack([k_b, heads(v_b, FOX_HEADS)], axis=2)
    logf = jax.nn.log_sigmoid(f_b.astype(jnp.float32) + fox_fbias.astype(jnp.float32))
    g_m = jax.nn.sigmoid(g_m).reshape(B, T, 2, D_MODEL)
    return q_a, nsa_rows, win_rows, g_a, q_b, fox_rows, logf, g_m


def compress(blocks, pe, w):
    m = jnp.mean((blocks + pe[None, None, :, None, :]).astype(jnp.float32), axis=2).astype(blocks.dtype)
    return jnp.einsum('bnkd,de->bnke', m, w)


def gather_grids(B):
    b_i = jnp.arange(B)[:, None, None, None, None, None]
    k_i = jnp.arange(NSA_KV_HEADS)[None, None, :, None, None, None]
    l_i = jnp.arange(CMP_BLOCK)[None, None, None, None, :, None]
    c_i = jnp.arange(2)[None, None, None, None, None, :]
    return b_i, k_i, l_i, c_i


def nsa_attend(q, t, kcb, vcb, fetch_sel, kw, vw, pos_w, gates):
    B, Q = q.shape[:2]
    scale = HEAD_DIM ** -0.5
    nb = kcb.shape[1]
    n = jnp.arange(nb)
    s_c = jnp.einsum('bqkgd,bnkd->bqkgn', q, kcb, preferred_element_type=jnp.float32) * scale
    valid_c = (n[None, :] + 1) * CMP_BLOCK - 1 <= t[:, None]
    p_c = masked_softmax(s_c, valid_c[None, :, None, None, :])
    o_c = jnp.einsum('bqkgn,bnkd->bqkgd', p_c.astype(vcb.dtype), vcb)
    cur = (t // CMP_BLOCK)[:, None]
    forced = (n[None, :] == cur) | (n[None, :] == 0)
    future = n[None, :] > cur
    imp = jnp.sum(p_c, axis=3)
    imp = jnp.where(forced[None, :, None, :], FORCED, jnp.where(future[None, :, None, :], -1.0, imp))
    top, idx = lax.top_k(imp, min(N_SELECT, nb))
    ksel, vsel = fetch_sel(idx)
    s_s = jnp.einsum('bqkgd,bqknld->bqkgnl', q, ksel, preferred_element_type=jnp.float32) * scale
    pos_s = idx[..., None] * CMP_BLOCK + jnp.arange(CMP_BLOCK)
    valid_s = (top[..., None] >= 0) & (pos_s <= t[None, :, None, None, None])
    ns = idx.shape[-1]
    p_s = masked_softmax(s_s.reshape(B, Q, NSA_KV_HEADS, NSA_GROUP, ns * CMP_BLOCK),
                         valid_s.reshape(B, Q, NSA_KV_HEADS, 1, ns * CMP_BLOCK))
    o_s = jnp.einsum('bqkgm,bqkmd->bqkgd', p_s.astype(vsel.dtype),
                     vsel.reshape(B, Q, NSA_KV_HEADS, ns * CMP_BLOCK, HEAD_DIM))
    s_w = jnp.einsum('bqkgd,blkd->bqkgl', q, kw, preferred_element_type=jnp.float32) * scale
    dpos = t[:, None] - pos_w[None, :]
    valid_w = (pos_w[None, :] >= 0) & (dpos >= 0) & (dpos < WINDOW)
    p_w = masked_softmax(s_w, valid_w[None, :, None, None, :])
    o_w = jnp.einsum('bqkgl,blkd->bqkgd', p_w.astype(vw.dtype), vw)
    g = gates.reshape(B, Q, NSA_KV_HEADS, NSA_GROUP, 3)
    o = g[..., 0:1] * o_c + g[..., 1:2] * o_s + g[..., 2:3] * o_w
    return o.reshape(B, Q, NSA_HEADS, HEAD_DIM)


def nsa_prompt(q, nsa_rows, win_rows, g_a, cmp_pe, w_cmp):
    B, T = q.shape[:2]
    nb = T // CMP_BLOCK
    nq = T // Q_BLOCK
    cmp_blocks = nsa_rows[:, :, 0:2].reshape(B, nb, CMP_BLOCK, 2, NSA_KV_HEADS, HEAD_DIM)
    kcb = compress(cmp_blocks[:, :, :, 0], cmp_pe[0], w_cmp[0])
    vcb = compress(cmp_blocks[:, :, :, 1], cmp_pe[1], w_cmp[1])
    sel_blocks = nsa_rows[:, :, 2:4].reshape(B, nb, CMP_BLOCK, 2, NSA_KV_HEADS, HEAD_DIM)
    win_pad = jnp.pad(win_rows, ((0, 0), (WINDOW, 0), (0, 0), (0, 0), (0, 0)))
    b_i, k_i, l_i, c_i = gather_grids(B)

    def fetch_sel(idx):
        r = sel_blocks[b_i, idx[..., None, None], l_i, c_i, k_i]
        return r[..., 0, :], r[..., 1, :]

    qg = q.reshape(B, nq, Q_BLOCK, NSA_KV_HEADS, NSA_GROUP, HEAD_DIM).transpose(1, 0, 2, 3, 4, 5)
    gg = g_a.reshape(B, nq, Q_BLOCK, NSA_HEADS, 3).transpose(1, 0, 2, 3, 4)

    def block(args):
        j, q_blk, g_blk = args
        t = j * Q_BLOCK + jnp.arange(Q_BLOCK, dtype=jnp.int32)
        w = lax.dynamic_slice_in_dim(win_pad, j * Q_BLOCK, WINDOW + Q_BLOCK, axis=1)
        pos_w = j * Q_BLOCK - WINDOW + jnp.arange(WINDOW + Q_BLOCK, dtype=jnp.int32)
        return nsa_attend(q_blk, t, kcb, vcb, fetch_sel, w[:, :, 0], w[:, :, 1], pos_w, g_blk)

    o = lax.map(block, (jnp.arange(nq, dtype=jnp.int32), qg, gg))
    return o.transpose(1, 0, 2, 3, 4).reshape(B, T, NSA_HEADS, HEAD_DIM)


def nsa_sample(q, nsa_new, win_new, g_a, cache_nsa, win_state, page_table, l, cmp_pe, w_cmp):
    DB, S = q.shape[:2]
    P = page_table.shape[1] * PAGE_SIZE
    t = P + jnp.arange(S, dtype=jnp.int32)
    past = cache_nsa[l, page_table, :, 0:2].reshape(DB, P, 2, NSA_KV_HEADS, HEAD_DIM)
    rows = jnp.concatenate([past, nsa_new[:, :, 0:2].astype(past.dtype)], axis=1)
    L = P + S
    nbs = -(-L // CMP_BLOCK)
    rows = jnp.pad(rows, ((0, 0), (0, nbs * CMP_BLOCK - L), (0, 0), (0, 0), (0, 0)))
    blocks = rows.reshape(DB, nbs, CMP_BLOCK, 2, NSA_KV_HEADS, HEAD_DIM)
    kcb = compress(blocks[:, :, :, 0], cmp_pe[0], w_cmp[0])
    vcb = compress(blocks[:, :, :, 1], cmp_pe[1], w_cmp[1])
    spp = PAGE_SIZE // CMP_BLOCK
    n_past_blk = P // CMP_BLOCK
    nt = -(-S // CMP_BLOCK)
    tail = jnp.pad(nsa_new[:, :, 2:4], ((0, 0), (0, nt * CMP_BLOCK - S), (0, 0), (0, 0), (0, 0)))
    tail = tail.reshape(DB, nt, CMP_BLOCK, 2, NSA_KV_HEADS, HEAD_DIM)
    b_i, k_i, l_i, c_i = gather_grids(DB)

    def fetch_sel(idx):
        in_past = idx < n_past_blk
        pn = jnp.minimum(idx, n_past_blk - 1)
        phys = page_table[b_i[..., 0, 0], pn // spp]
        offs = (pn % spp) * CMP_BLOCK
        from_pool = cache_nsa[l, phys[..., None, None], offs[..., None, None] + l_i, 2 + c_i, k_i]
        tn = jnp.clip(idx - n_past_blk, 0, nt - 1)
        from_tail = tail[b_i, tn[..., None, None], l_i, c_i, k_i]
        r = jnp.where(in_past[..., None, None, None], from_pool, from_tail.astype(from_pool.dtype))
        return r[..., 0, :], r[..., 1, :]

    wb = win_state.shape[1]
    win_all = jnp.concatenate([win_state, win_new.astype(win_state.dtype)], axis=1)
    pos_w = P - wb + jnp.arange(wb + S, dtype=jnp.int32)
    qg = q.reshape(DB, S, NSA_KV_HEADS, NSA_GROUP, HEAD_DIM)
    o = nsa_attend(qg, t, kcb, vcb, fetch_sel, win_all[:, :, 0], win_all[:, :, 1], pos_w, g_a)
    return o, win_all[:, S:]


def fox_prompt(q, fox_rows, logf):
    B, T, H, D = q.shape
    k, v = fox_rows[:, :, 0], fox_rows[:, :, 1]
    cT = lax.cumsum(logf, axis=1).transpose(0, 2, 1)
    nq = T // Q_BLOCK
    qg = q.reshape(B, nq, Q_BLOCK, H, D).transpose(1, 0, 2, 3, 4)
    s_pos = jnp.arange(T, dtype=jnp.int32)
    scale = HEAD_DIM ** -0.5

    def block(args):
        j, q_blk = args
        t = j * Q_BLOCK + jnp.arange(Q_BLOCK, dtype=jnp.int32)
        c_q = lax.dynamic_slice_in_dim(cT, j * Q_BLOCK, Q_BLOCK, axis=2)
        s = jnp.einsum('bqhd,bshd->bhqs', q_blk, k, preferred_element_type=jnp.float32) * scale
        s = s + c_q[..., None] - cT[:, :, None, :]
        p = masked_softmax(s, (s_pos[None, :] <= t[:, None])[None, None])
        return jnp.einsum('bhqs,bshd->bqhd', p.astype(v.dtype), v)

    o = lax.map(block, (jnp.arange(nq, dtype=jnp.int32), qg))
    return o.transpose(1, 0, 2, 3, 4).reshape(B, T, H, D)


def fox_sample(q, fox_new, logf_new, cache_fox, cache_logf, page_table, l):
    DB, S, H, D = q.shape
    P = page_table.shape[1] * PAGE_SIZE
    k_past = cache_fox[l, page_table, :, 0].reshape(DB, P, H, D)
    v_past = cache_fox[l, page_table, :, 1].reshape(DB, P, H, D)
    lf_past = cache_logf[l, page_table].reshape(DB, P, H).astype(jnp.float32)
    suffix = lax.cumsum(lf_past, axis=1, reverse=True) - lf_past
    c_new = lax.cumsum(logf_new, axis=1)
    cnT = c_new.transpose(0, 2, 1)
    scale = HEAD_DIM ** -0.5
    s_past = jnp.einsum('bqhd,bshd->bhqs', q, k_past, preferred_element_type=jnp.float32) * scale
    s_past = s_past + cnT[..., None] + suffix.transpose(0, 2, 1)[:, :, None, :]
    s_new = jnp.einsum('bqhd,bshd->bhqs', q, fox_new[:, :, 0], preferred_element_type=jnp.float32) * scale
    s_new = s_new + cnT[..., None] - cnT[:, :, None, :]
    causal = jnp.arange(S)[None, :] <= jnp.arange(S)[:, None]
    s_new = jnp.where(causal[None, None], s_new, NEG_INF)
    p = jax.nn.softmax(jnp.concatenate([s_past, s_new], axis=-1), axis=-1)
    o = jnp.einsum('bhqs,bshd->bqhd', p[..., :P].astype(v_past.dtype), v_past)
    o = o + jnp.einsum('bhqs,bshd->bqhd', p[..., P:].astype(fox_new.dtype), fox_new[:, :, 1])
    return o


def merge_and_ffn(x, o_a, o_b, g_m, p_i, w_up_nsa, w_up_fox, w_out, norm_ffn,
                  w_ffn_gate, w_ffn_up, w_ffn_down, norm_ple, w_ple_gate, w_ple_proj):
    B, T = x.shape[:2]
    u_a = jnp.einsum('bte,ed->btd', o_a.reshape(B, T, -1), w_up_nsa)
    u_b = jnp.einsum('bte,ed->btd', o_b.reshape(B, T, -1), w_up_fox)
    mixed = g_m[:, :, 0] * u_a + g_m[:, :, 1] * u_b
    x = x + jnp.einsum('btd,de->bte', mixed, w_out)
    h = rms_norm(x, norm_ffn)
    ff = jax.nn.silu(jnp.einsum('btd,df->btf', h, w_ffn_gate)) * jnp.einsum('btd,df->btf', h, w_ffn_up)
    x = x + jnp.einsum('btf,fd->btd', ff, w_ffn_down)
    gate = jax.nn.sigmoid(jnp.einsum('btd,de->bte', rms_norm(x, norm_ple), w_ple_gate))
    return x + gate * jnp.einsum('btp,pd->btd', p_i, w_ple_proj)


def setup_inputs(seed: int = 0) -> dict:
    key = jax.random.key(seed)
    keys = iter(jax.random.split(key, 48))
    nk = lambda: next(keys)
    f32 = jnp.float32
    n_pages = PAST_LEN // PAGE_SIZE
    used = DEC_BATCH * n_pages
    n_pool = used + max(1, used // 4)
    win_buf = min(WINDOW, PAST_LEN)
    w = lambda shape, fan_in: jax.random.normal(nk(), shape, f32) * fan_in ** -0.5
    gain = lambda shape: 1.0 + 0.05 * jax.random.normal(nk(), shape, f32)
    return {
        "x_prompt": jax.random.normal(nk(), (BATCH, SEQ, D_MODEL), f32),
        "x_sample": jax.random.normal(nk(), (DEC_BATCH, DEC_SEQ, D_MODEL), f32),
        "p_prompt": jax.random.normal(nk(), (DEPTH, BATCH, SEQ, PLE_DIM), f32),
        "p_sample": jax.random.normal(nk(), (DEPTH, DEC_BATCH, DEC_SEQ, PLE_DIM), f32),
        "cache_nsa": jax.random.normal(nk(), (DEPTH, n_pool, PAGE_SIZE, 4, NSA_KV_HEADS, HEAD_DIM), f32),
        "cache_fox": jax.random.normal(nk(), (DEPTH, n_pool, PAGE_SIZE, 2, FOX_HEADS, HEAD_DIM), f32),
        "cache_fox_logf": jax.nn.log_sigmoid(4.0 + jax.random.normal(nk(), (DEPTH, n_pool, PAGE_SIZE, FOX_HEADS), f32)),
        "state_win": jax.random.normal(nk(), (DEPTH, DEC_BATCH, win_buf, 2, NSA_KV_HEADS, HEAD_DIM), f32),
        "page_table": jax.random.permutation(nk(), n_pool)[:used].reshape(DEC_BATCH, n_pages).astype(jnp.int32),
        "norm_mix": gain((DEPTH, D_MODEL)),
        "w_in": w((DEPTH, D_MODEL, IN_WIDTH), D_MODEL),
        "q_norm_nsa": gain((DEPTH, HEAD_DIM)),
        "k_norm_nsa": gain((DEPTH, 3, HEAD_DIM)),
        "q_norm_fox": gain((DEPTH, HEAD_DIM)),
        "k_norm_fox": gain((DEPTH, HEAD_DIM)),
        "fox_fbias": jnp.linspace(1.0, 5.0, FOX_HEADS, dtype=f32)[None, :] + 0.1 * jax.random.normal(nk(), (DEPTH, FOX_HEADS), f32),
        "cmp_pe": 0.5 * jax.random.normal(nk(), (DEPTH, 2, CMP_BLOCK, HEAD_DIM), f32),
        "w_cmp": w((DEPTH, 2, HEAD_DIM, HEAD_DIM), HEAD_DIM),
        "w_up_nsa": w((DEPTH, NSA_HEADS * HEAD_DIM, D_MODEL), NSA_HEADS * HEAD_DIM),
        "w_up_fox": w((DEPTH, FOX_HEADS * HEAD_DIM, D_MODEL), FOX_HEADS * HEAD_DIM),
        "w_out": w((DEPTH, D_MODEL, D_MODEL), D_MODEL),
        "norm_ffn": gain((DEPTH, D_MODEL)),
        "w_ffn_gate": w((DEPTH, D_MODEL, D_FF), D_MODEL),
        "w_ffn_up": w((DEPTH, D_MODEL, D_FF), D_MODEL),
        "w_ffn_down": w((DEPTH, D_FF, D_MODEL), D_FF),
        "norm_ple": gain((DEPTH, D_MODEL)),
        "w_ple_gate": w((DEPTH, D_MODEL, D_MODEL), D_MODEL),
        "w_ple_proj": w((DEPTH, PLE_DIM, D_MODEL), PLE_DIM),
    }


def reference(x_prompt, x_sample, p_prompt, p_sample, cache_nsa, cache_fox, cache_fox_logf, state_win,
              page_table, norm_mix, w_in, q_norm_nsa, k_norm_nsa, q_norm_fox, k_norm_fox, fox_fbias,
              cmp_pe, w_cmp, w_up_nsa, w_up_fox, w_out, norm_ffn, w_ffn_gate, w_ffn_up, w_ffn_down,
              norm_ple, w_ple_gate, w_ple_proj):
    T = x_prompt.shape[1]
    S = x_sample.shape[1]
    P = page_table.shape[1] * PAGE_SIZE
    pos_p = jnp.arange(T, dtype=jnp.int32)
    pos_s = P + jnp.arange(S, dtype=jnp.int32)
    win_keep = min(WINDOW, T)
    xp, xs = x_prompt, x_sample
    nsa_p, nsa_s, fox_p, fox_s, lf_p, lf_s, win_p, win_s = [], [], [], [], [], [], [], []
    for l in range(DEPTH):
        tail_w = (w_up_nsa[l], w_up_fox[l], w_out[l], norm_ffn[l], w_ffn_gate[l], w_ffn_up[l],
                  w_ffn_down[l], norm_ple[l], w_ple_gate[l], w_ple_proj[l])
        h = rms_norm(xp, norm_mix[l])
        q_a, nsa_rows, win_rows, g_a, q_b, fox_rows, logf, g_m = mixer_inputs(
            h, pos_p, w_in[l], q_norm_nsa[l], k_norm_nsa[l], q_norm_fox[l], k_norm_fox[l], fox_fbias[l])
        o_a = nsa_prompt(q_a, nsa_rows, win_rows, g_a, cmp_pe[l], w_cmp[l])
        o_b = fox_prompt(q_b, fox_rows, logf)
        xp = merge_and_ffn(xp, o_a, o_b, g_m, p_prompt[l], *tail_w)
        nsa_p.append(nsa_rows)
        fox_p.append(fox_rows)
        lf_p.append(logf)
        win_p.append(win_rows[:, T - win_keep:])
        h = rms_norm(xs, norm_mix[l])
        q_a, nsa_rows, win_rows, g_a, q_b, fox_rows, logf, g_m = mixer_inputs(
            h, pos_s, w_in[l], q_norm_nsa[l], k_norm_nsa[l], q_norm_fox[l], k_norm_fox[l], fox_fbias[l])
        o_a, new_win = nsa_sample(q_a, nsa_rows, win_rows, g_a, cache_nsa, state_win[l], page_table, l,
                                  cmp_pe[l], w_cmp[l])
        o_b = fox_sample(q_b, fox_rows, logf, cache_fox, cache_fox_logf, page_table, l)
        xs = merge_and_ffn(xs, o_a, o_b, g_m, p_sample[l], *tail_w)
        nsa_s.append(nsa_rows)
        fox_s.append(fox_rows)
        lf_s.append(logf)
        win_s.append(new_win)
    return (xp, xs, jnp.stack(nsa_p), jnp.stack(nsa_s), jnp.stack(fox_p), jnp.stack(fox_s),
            jnp.stack(lf_p), jnp.stack(lf_s), jnp.stack(win_p), jnp.stack(win_s))
```

```python
import functools

import jax
import jax.numpy as jnp
import numpy as np
from jax import lax
from jax.experimental import pallas as pl
from jax.experimental.pallas import tpu as pltpu

HEAD_DIM = 64
NSA_HEADS = 8
NSA_KV_HEADS = 2
NSA_GROUP = NSA_HEADS // NSA_KV_HEADS
FOX_HEADS = 8
CMP_BLOCK = 64
N_SELECT = 16
WINDOW = 512
PAGE_SIZE = 128
ROPE_THETA = 10000.0
EPS = 1e-6
NEG_INF = -1e30
FORCED = 1e4
SCALE = HEAD_DIM ** -0.5

LANES = 128
VMEM_LIMIT = 56 * 1024 * 1024
UNSELECTED = -30000.0

F32 = jnp.float32
BF16 = jnp.bfloat16

_R_QA, _R_KV6, _R_QB, _R_KB, _R_VB, _R_FB, _R_END = 0, 512, 1280, 1792, 2304, 2816, 2832


def _nt(a, b):
    return lax.dot_general(a, b, (((1,), (1,)), ((), ())), preferred_element_type=F32)


def _dot(a, b):
    return jnp.dot(a, b, preferred_element_type=F32)


def _split3(x):
    a = x.astype(BF16)
    r = x - a.astype(F32)
    b = r.astype(BF16)
    c = (r - b.astype(F32)).astype(BF16)
    return a, b, c


def _log_sigmoid(x):
    return jnp.minimum(x, 0.0) - jnp.log1p(jnp.exp(-jnp.abs(x)))


def _cparams(n_grid):
    return pltpu.CompilerParams(dimension_semantics=("arbitrary",) * n_grid,
                                vmem_limit_bytes=VMEM_LIMIT)


def _proj_kernel(x_ref, gmix_ref, wt_ref, wg_ref, cos_ref, sin_ref, hg_ref, fb_ref,
                 nsaT_ref, winT_ref, foxT_ref, logfT_ref, qn_ref, qf_ref, kf_ref, kvb_ref,
                 ga_ref, gm_ref, qT_ref, carry_ref, *, tm):
    j = pl.program_id(1)

    @pl.when(j == 0)
    def _():
        carry_ref[...] = jnp.zeros_like(carry_ref)

    x = x_ref[0]
    ms = jnp.mean(x * x, axis=-1, keepdims=True)
    h = (x * lax.rsqrt(ms + EPS) * gmix_ref[...]).astype(BF16)
    cos = cos_ref[...]
    sin = sin_ref[...]
    hg = hg_ref[...]

    def zt(r0, r1):
        return _nt(wt_ref[r0:r1, :], h)

    def headnorm(z, col):
        ss = jnp.sum(z * z, axis=0, keepdims=True)
        return z * lax.rsqrt(ss * (1.0 / HEAD_DIM) + EPS) * hg[:, col:col + 1]

    def rope(z):
        x1, x2 = z[:HEAD_DIM // 2], z[HEAD_DIM // 2:]
        return jnp.concatenate([x1 * cos - x2 * sin, x2 * cos + x1 * sin], axis=0)

    zeros64 = jnp.zeros((HEAD_DIM, tm), F32)
    row8 = lax.broadcasted_iota(jnp.int32, (8, tm), 0)

    for hp in range(NSA_HEADS // 2):
        z = zt(_R_QA + hp * 128, _R_QA + (hp + 1) * 128)
        for i in range(2):
            hd = hp * 2 + i
            q = rope(headnorm(z[i * 64:(i + 1) * 64], 0))
            qT_ref[0, hd * 64:(hd + 1) * 64, :] = q
            blk = jnp.concatenate([q * SCALE, zeros64], axis=0)
            qn_ref[0, hd] = blk.T.astype(BF16)

    for part in range(6):
        z = zt(_R_KV6 + part * 128, _R_KV6 + (part + 1) * 128)
        if part % 2 == 0:
            z = jnp.concatenate([rope(headnorm(z[:64], 1 + part // 2)),
                                 rope(headnorm(z[64:], 1 + part // 2))], axis=0)
        if part < 4:
            nsaT_ref[0, part * 128:(part + 1) * 128, :] = z
        else:
            winT_ref[0, (part - 4) * 128:(part - 3) * 128, :] = z
        if part >= 2:
            kvb_ref[0, (part - 2) * 128:(part - 1) * 128, :] = z.astype(BF16)

    zf = zt(_R_FB, _R_END)[:8]
    logf = _log_sigmoid(zf + fb_ref[...])
    logfT_ref[0] = logf
    ii = lax.broadcasted_iota(jnp.int32, (tm, tm), 0)
    jj = lax.broadcasted_iota(jnp.int32, (tm, tm), 1)
    tri = jnp.where(ii <= jj, 1.0, 0.0).astype(BF16)
    l1, l2, l3 = _split3(logf)
    c = carry_ref[:, 0:1] + (_dot(l1, tri) + _dot(l2, tri) + _dot(l3, tri))
    carry_ref[...] = jnp.broadcast_to(c[:, tm - 1:tm], carry_ref.shape)
    c1, c2, c3 = [p.astype(F32) for p in _split3(c)]

    for hp in range(FOX_HEADS // 2):
        z = zt(_R_QB + hp * 128, _R_QB + (hp + 1) * 128)
        for i in range(2):
            hd = hp * 2 + i
            q = headnorm(z[i * 64:(i + 1) * 64], 4)
            qT_ref[0, 512 + hd * 64:512 + (hd + 1) * 64, :] = q
            aug = jnp.where(row8 == 0, c1[hd:hd + 1],
                            jnp.where(row8 == 1, c2[hd:hd + 1],
                                      jnp.where(row8 == 2, c3[hd:hd + 1],
                                                jnp.where(row8 < 6, 1.0, 0.0))))
            blk = jnp.concatenate([q * SCALE, aug, jnp.zeros((56, tm), F32)], axis=0)
            qf_ref[0, hd] = blk.T.astype(BF16)

    for hp in range(FOX_HEADS // 2):
        z = zt(_R_KB + hp * 128, _R_KB + (hp + 1) * 128)
        for i in range(2):
            hd = hp * 2 + i
            k = headnorm(z[i * 64:(i + 1) * 64], 5)
            foxT_ref[0, hd * 64:(hd + 1) * 64, :] = k
            aug = jnp.where(row8 < 3, 1.0,
                            jnp.where(row8 == 3, -c1[hd:hd + 1],
                                      jnp.where(row8 == 4, -c2[hd:hd + 1],
                                                jnp.where(row8 == 5, -c3[hd:hd + 1], 0.0))))
            blk = jnp.concatenate([k, aug, jnp.zeros((56, tm), F32)], axis=0)
            kf_ref[0, hd] = blk.astype(BF16)

    for hp in range(FOX_HEADS // 2):
        z = zt(_R_VB + hp * 128, _R_VB + (hp + 1) * 128)
        foxT_ref[0, 512 + hp * 128:512 + (hp + 1) * 128, :] = z
        kvb_ref[0, 512 + hp * 128:512 + (hp + 1) * 128, :] = z.astype(BF16)

    zg = _dot(h, wg_ref[:, 0:256])
    ga_ref[0, 0] = jax.nn.sigmoid(zg[:, 0:128])
    ga_ref[0, 1] = jax.nn.sigmoid(zg[:, 128:256])
    ngm = gm_ref.shape[-1]
    for c0 in range(0, ngm, 512):
        gm_ref[0, :, c0:c0 + 512] = jax.nn.sigmoid(_dot(h, wg_ref[:, 256 + c0:256 + c0 + 512]))


def _proj(x, gmix, wt, wg, cos, sin, hg, fb, tm):
    B, T, D = x.shape
    grid = (B, T // tm)
    ngm = wg.shape[1] - 256
    const = lambda *shape: pl.BlockSpec(shape, lambda b, j: (0,) * len(shape))
    out_shape = [
        jax.ShapeDtypeStruct((B, 512, T), F32),
        jax.ShapeDtypeStruct((B, 256, T), F32),
        jax.ShapeDtypeStruct((B, 1024, T), F32),
        jax.ShapeDtypeStruct((B, 8, T), F32),
        jax.ShapeDtypeStruct((B, NSA_HEADS, T, 128), BF16),
        jax.ShapeDtypeStruct((B, FOX_HEADS, T, 128), BF16),
        jax.ShapeDtypeStruct((B, FOX_HEADS, 128, T), BF16),
        jax.ShapeDtypeStruct((B, 1024, T), BF16),
        jax.ShapeDtypeStruct((B, 2, T, 128), F32),
        jax.ShapeDtypeStruct((B, T, ngm), F32),
        jax.ShapeDtypeStruct((B, 1024, T), F32),
    ]
    out_specs = [
        pl.BlockSpec((1, 512, tm), lambda b, j: (b, 0, j)),
        pl.BlockSpec((1, 256, tm), lambda b, j: (b, 0, j)),
        pl.BlockSpec((1, 1024, tm), lambda b, j: (b, 0, j)),
        pl.BlockSpec((1, 8, tm), lambda b, j: (b, 0, j)),
        pl.BlockSpec((1, NSA_HEADS, tm, 128), lambda b, j: (b, 0, j, 0)),
        pl.BlockSpec((1, FOX_HEADS, tm, 128), lambda b, j: (b, 0, j, 0)),
        pl.BlockSpec((1, FOX_HEADS, 128, tm), lambda b, j: (b, 0, 0, j)),
        pl.BlockSpec((1, 1024, tm), lambda b, j: (b, 0, j)),
        pl.BlockSpec((1, 2, tm, 128), lambda b, j: (b, 0, j, 0)),
        pl.BlockSpec((1, tm, ngm), lambda b, j: (b, j, 0)),
        pl.BlockSpec((1, 1024, tm), lambda b, j: (b, 0, j)),
    ]
    in_specs = [
        pl.BlockSpec((1, tm, D), lambda b, j: (b, j, 0)),
        const(1, D),
        const(*wt.shape),
        const(*wg.shape),
        pl.BlockSpec((32, tm), lambda b, j: (0, j)),
        pl.BlockSpec((32, tm), lambda b, j: (0, j)),
        const(64, 8),
        const(8, 1),
    ]
    return pl.pallas_call(
        functools.partial(_proj_kernel, tm=tm),
        grid=grid, in_specs=in_specs, out_specs=out_specs, out_shape=out_shape,
        scratch_shapes=[pltpu.VMEM((8, 128), F32)],
        compiler_params=_cparams(2), name="proj",
    )(x, gmix, wt, wg, cos, sin, hg, fb)


def _fox_kernel(q_ref, k_ref, v_ref, o_ref, *, tq):
    j = pl.program_id(2)
    row = lax.broadcasted_iota(jnp.int32, (128, tq), 0)
    lane_o = lax.broadcasted_iota(jnp.int32, (tq, 128), 1)
    rr = lax.broadcasted_iota(jnp.int32, (tq, tq), 0)
    cc = lax.broadcasted_iota(jnp.int32, (tq, tq), 1)
    outs = []
    for i in range(2):
        q = q_ref[0, i]
        keep = jnp.where((row < 64) if i == 0 else (row >= 64), 1.0, 0.0).astype(BF16)
        one_row = 64 if i == 0 else 0
        fill = jnp.where(row == one_row, 1.0, 0.0).astype(BF16)

        def step(kt, carry, diag):
            m, acc = carry
            ks = k_ref[0, i, :, pl.ds(pl.multiple_of(kt * tq, tq), tq)]
            s = _dot(q, ks)
            if diag:
                s = jnp.where(cc <= rr, s, NEG_INF)
            m_new = jnp.maximum(m, jnp.max(s, axis=1, keepdims=True))
            p = jnp.exp(s - m_new).astype(BF16)
            vs = v_ref[0, :, pl.ds(pl.multiple_of(kt * tq, tq), tq)]
            vm = vs * keep + fill
            acc = jnp.exp(m - m_new) * acc + _nt(p, vm)
            return m_new, acc

        m0 = jnp.full((tq, 1), NEG_INF, F32)
        a0 = jnp.zeros((tq, 128), F32)
        carry = lax.fori_loop(0, j, functools.partial(step, diag=False), (m0, a0))
        _, acc = step(j, carry, True)
        outs.append(acc / acc[:, one_row:one_row + 1])
    o_ref[0] = jnp.where(lane_o < 64, outs[0], outs[1])


def _fox_prompt(qf, kf, kvb, tq):
    B, H, T, _ = qf.shape
    grid = (B, H // 2, T // tq)
    return pl.pallas_call(
        functools.partial(_fox_kernel, tq=tq),
        grid=grid,
        in_specs=[
            pl.BlockSpec((1, 2, tq, 128), lambda b, hp, j: (b, hp, j, 0)),
            pl.BlockSpec((1, 2, 128, T), lambda b, hp, j: (b, hp, 0, 0)),
            pl.BlockSpec((1, 128, T), lambda b, hp, j: (b, 4 + hp, 0)),
        ],
        out_specs=pl.BlockSpec((1, tq, 128), lambda b, hp, j: (b, j, hp)),
        out_shape=jax.ShapeDtypeStruct((B, T, 512), F32),
        compiler_params=_cparams(3), name="fox_prompt",
    )(qf, kf, kvb)


def _cmp_kernel(xk_ref, xv_ref, avg_ref, avgT_ref, peT_ref, pe_ref, w0T_ref, w1_ref,
                kcb_ref, vcb_ref):
    lane = lax.broadcasted_iota(jnp.int32, (128, 128), 1)
    rowi = lax.broadcasted_iota(jnp.int32, (128, 128), 0)
    xk = xk_ref[0]
    k1, k2, _ = _split3(xk)
    mk = _dot(k1, avg_ref[...]) + _dot(k2, avg_ref[...])
    mk = mk + jnp.mean(peT_ref[0], axis=1, keepdims=True)
    kcb = _dot(w0T_ref[...], mk.astype(BF16))
    kcb_ref[0, 0] = jnp.where(lane >= 64, kcb, 0.0).astype(BF16)
    xv = xv_ref[0]
    v1, v2, _ = _split3(xv)
    mv = _nt(avgT_ref[...], v1) + _nt(avgT_ref[...], v2)
    mv = mv + jnp.mean(pe_ref[1], axis=0, keepdims=True)
    vcb = _dot(mv.astype(BF16), w1_ref[...])
    vcb_ref[0, 0] = jnp.where(rowi >= 64, vcb, 0.0).astype(BF16)


def _cmp_prompt(nsaT, avg, avgT, peT, pe, w0T, w1):
    B, _, T = nsaT.shape
    const = lambda *shape: pl.BlockSpec(shape, lambda b, kv: (0,) * len(shape))
    return pl.pallas_call(
        _cmp_kernel,
        grid=(B, NSA_KV_HEADS),
        in_specs=[
            pl.BlockSpec((1, 64, T), lambda b, kv: (b, kv, 0)),
            pl.BlockSpec((1, 64, T), lambda b, kv: (b, 2 + kv, 0)),
            const(T, 128), const(128, T), const(2, 64, 64), const(2, 64, 64),
            const(128, 64), const(64, 128),
        ],
        out_specs=[pl.BlockSpec((1, 1, 128, 128), lambda b, kv: (b, kv, 0, 0))] * 2,
        out_shape=[jax.ShapeDtypeStruct((B, NSA_KV_HEADS, 128, 128), BF16)] * 2,
        compiler_params=_cparams(2), name="cmp_prompt",
    )(nsaT, nsaT, avg, avgT, peT, pe, w0T, w1)


def _nsa_kernel(q_ref, kcb_ref, vcb_ref, ks_ref, vs_ref, kw_ref, vw_ref, oh_ref, ga_ref,
                o_ref, *, nb):
    j = pl.program_id(2)
    QB = 128
    R = NSA_GROUP * QB
    q4 = q_ref[0].reshape(R, 128)
    lane = lax.broadcasted_iota(jnp.int32, (QB, 128), 1)
    n_idx = lane - 64
    r_q = lax.broadcasted_iota(jnp.int32, (QB, 128), 0)
    t = j * QB + r_q
    real = (lane >= 64) & (n_idx < nb)
    rep4 = lambda a: jnp.concatenate([a] * NSA_GROUP, axis=0)
    lane4 = lax.broadcasted_iota(jnp.int32, (R, 128), 1)
    t4 = j * QB + lax.broadcasted_iota(jnp.int32, (R, 128), 0) % QB

    s_c = _dot(q4, kcb_ref[0, 0])
    valid_c = (lane4 >= 64) & (lane4 - 64 < nb) & ((lane4 - 63) * CMP_BLOCK - 1 <= t4)
    s = jnp.where(valid_c, s_c, NEG_INF)
    e = jnp.exp(s - jnp.max(s, axis=1, keepdims=True))
    p_c = jnp.where(valid_c, e / jnp.sum(e, axis=1, keepdims=True), 0.0)
    o_c = _dot(p_c.astype(BF16), vcb_ref[0, 0])

    imp = p_c[0:QB] + p_c[QB:2 * QB] + p_c[2 * QB:3 * QB] + p_c[3 * QB:4 * QB]
    cur = t // CMP_BLOCK
    imp = jnp.where((n_idx == cur) | (n_idx == 0), FORCED, jnp.where(n_idx > cur, -1.0, imp))
    imp = jnp.where(real, imp, -2.0)
    rank = jnp.zeros((QB, 128), jnp.int32)
    for mm in range(nb):
        col = imp[:, 64 + mm:65 + mm]
        beats = (col > imp) | ((col == imp) & (n_idx > mm))
        rank = rank + beats.astype(jnp.int32)
    sel = real & (rank < N_SELECT) & (n_idx <= cur)
    bias = jnp.where(lane >= 64, jnp.where(sel, 0.0, UNSELECTED), 0.0)
    q_sel = (q4.astype(F32) + rep4(bias)).astype(BF16)

    row64 = lax.broadcasted_iota(jnp.int32, (64, QB), 0)
    ones_blk = jnp.where(row64 == 0, 1.0, 0.0).astype(BF16)
    rr = lax.broadcasted_iota(jnp.int32, (R, QB), 0) % QB
    cc = lax.broadcasted_iota(jnp.int32, (R, QB), 1)

    def attend(q, k_ref_, v_ref_, kt, mask, carry):
        m, acc = carry
        off = pl.multiple_of(kt * QB, QB)
        k_aug = jnp.concatenate([k_ref_[0, :, pl.ds(off, QB)], oh_ref[:, pl.ds(off, QB)]], axis=0)
        s = _dot(q, k_aug)
        if mask is not None:
            s = jnp.where(mask, s, NEG_INF)
        m_new = jnp.maximum(m, jnp.max(s, axis=1, keepdims=True))
        p = jnp.exp(s - m_new).astype(BF16)
        v_aug = jnp.concatenate([v_ref_[0, :, pl.ds(off, QB)], ones_blk], axis=0)
        acc = jnp.exp(m - m_new) * acc + _nt(p, v_aug)
        return m_new, acc

    init = (jnp.full((R, 1), NEG_INF, F32), jnp.zeros((R, 128), F32))

    carry = lax.fori_loop(0, j, lambda kt, c: attend(q_sel, ks_ref, vs_ref, kt, None, c), init)
    _, acc_s = attend(q_sel, ks_ref, vs_ref, j, cc <= rr, carry)
    o_s = acc_s / acc_s[:, 64:65]

    def win_step(kt, c):
        dpos = (j - kt) * QB + rr - cc
        return attend(q4, kw_ref, vw_ref, kt, (dpos >= 0) & (dpos < WINDOW), c)

    _, acc_w = lax.fori_loop(jnp.maximum(j - WINDOW // QB, 0), j + 1, win_step, init)
    o_w = acc_w / acc_w[:, 64:65]

    ga = ga_ref[0, 0]
    outs = []
    for g in range(NSA_GROUP):
        sl = slice(g * QB, (g + 1) * QB)
        outs.append(ga[:, 3 * g:3 * g + 1] * o_c[sl] + ga[:, 3 * g + 1:3 * g + 2] * o_s[sl]
                    + ga[:, 3 * g + 2:3 * g + 3] * o_w[sl])
    lo = lane < 64
    o_ref[0, :, 0:128] = jnp.where(lo, outs[0], pltpu.roll(outs[1], 64, 1))
    o_ref[0, :, 128:256] = jnp.where(lo, outs[2], pltpu.roll(outs[3], 64, 1))


def _nsa_prompt(qn, kcb, vcb, kvb, onehot, ga):
    B, H, T, _ = qn.shape
    nb = T // CMP_BLOCK
    assert nb <= 64
    QB = 128
    grid = (B, NSA_KV_HEADS, T // QB)
    kvspec = lambda base: pl.BlockSpec((1, 64, T), lambda b, kv, j: (b, base + kv, 0))
    return pl.pallas_call(
        functools.partial(_nsa_kernel, nb=nb),
        grid=grid,
        in_specs=[
            pl.BlockSpec((1, NSA_GROUP, QB, 128), lambda b, kv, j: (b, kv, j, 0)),
            pl.BlockSpec((1, 1, 128, 128), lambda b, kv, j: (b, kv, 0, 0)),
            pl.BlockSpec((1, 1, 128, 128), lambda b, kv, j: (b, kv, 0, 0)),
            kvspec(0), kvspec(2), kvspec(4), kvspec(6),
            pl.BlockSpec((64, T), lambda b, kv, j: (0, 0)),
            pl.BlockSpec((1, 1, QB, 128), lambda b, kv, j: (b, kv, j, 0)),
        ],
        out_specs=pl.BlockSpec((1, QB, 256), lambda b, kv, j: (b, j, kv)),
        out_shape=jax.ShapeDtypeStruct((B, T, 512), F32),
        compiler_params=_cparams(3), name="nsa_prompt",
    )(qn, kcb, vcb, kvb, kvb, kvb, kvb, onehot, ga)


def _tail_kernel(x_ref, oa_ref, ob_ref, gm_ref, p_ref, wua_ref, wub_ref, wo_ref, nf_ref,
                 wg_ref, wu_ref, wd_ref, npl_ref, wpg_ref, wpp_ref, y_ref):
    D = x_ref.shape[-1]
    u_a = _dot(oa_ref[...].astype(BF16), wua_ref[...])
    u_b = _dot(ob_ref[...].astype(BF16), wub_ref[...])
    mixed = gm_ref[:, 0:D] * u_a + gm_ref[:, D:2 * D] * u_b
    x1 = x_ref[...] + _dot(mixed.astype(BF16), wo_ref[...])

    def rms(v, g):
        return (v * lax.rsqrt(jnp.mean(v * v, axis=-1, keepdims=True) + EPS) * g).astype(BF16)

    h = rms(x1, nf_ref[...])
    ff = jax.nn.silu(_dot(h, wg_ref[...])) * _dot(h, wu_ref[...])
    x2 = x1 + _dot(ff.astype(BF16), wd_ref[...])
    gate = jax.nn.sigmoid(_dot(rms(x2, npl_ref[...]), wpg_ref[...]))
    y_ref[...] = x2 + gate * _dot(p_ref[...].astype(BF16), wpp_ref[...])


def _tail(x, oa, ob, gm, p, w, tm):
    N, D = x.shape
    tok = lambda width: pl.BlockSpec((tm, width), lambda i: (i, 0))
    const = lambda a: pl.BlockSpec(a.shape, lambda i: (0,) * a.ndim, pipeline_mode=pl.Buffered(1))
    return pl.pallas_call(
        _tail_kernel,
        grid=(N // tm,),
        in_specs=[tok(D), tok(oa.shape[1]), tok(ob.shape[1]), tok(gm.shape[1]), tok(p.shape[1])]
        + [const(a) for a in w],
        out_specs=tok(D),
        out_shape=jax.ShapeDtypeStruct((N, D), F32),
        compiler_params=_cparams(1), name="tail",
    )(x, oa, ob, gm, p, *w)


def _prep_layer(l, norm_mix, w_in, q_norm_nsa, k_norm_nsa, q_norm_fox, k_norm_fox, fox_fbias,
                cmp_pe, w_cmp, w_up_nsa, w_up_fox, w_out, norm_ffn, w_ffn_gate, w_ffn_up,
                w_ffn_down, norm_ple, w_ple_gate, w_ple_proj):
    wl = w_in[l]
    D = wl.shape[0]
    o_ga = 512 + 6 * 128
    o_qb = o_ga + 3 * NSA_HEADS
    o_fb = o_qb + 3 * 512
    o_gm = o_fb + FOX_HEADS
    wt = jnp.concatenate([wl[:, 0:o_ga], wl[:, o_qb:o_gm], jnp.zeros((D, 8), F32)], axis=1).T.astype(BF16)
    zpad = jnp.zeros((D, 128 - 12), F32)
    wg = jnp.concatenate([wl[:, o_ga:o_ga + 12], zpad, wl[:, o_ga + 12:o_ga + 24], zpad, wl[:, o_gm:]],
                         axis=1).astype(BF16)
    hg = jnp.stack([q_norm_nsa[l], k_norm_nsa[l, 0], k_norm_nsa[l, 1], k_norm_nsa[l, 2],
                    q_norm_fox[l], k_norm_fox[l], jnp.zeros((64,), F32), jnp.zeros((64,), F32)], axis=1)
    z64 = jnp.zeros((64, 64), F32)
    return dict(
        gmix=norm_mix[l][None, :], wt=wt, wg=wg, hg=hg, fb=fox_fbias[l][:, None],
        pe=cmp_pe[l], peT=jnp.swapaxes(cmp_pe[l], 1, 2),
        w0T=jnp.concatenate([w_cmp[l, 0].T, z64], axis=0).astype(BF16),
        w1=jnp.concatenate([w_cmp[l, 1], z64], axis=1).astype(BF16),
        w_cmp=w_cmp[l],
        tail=(w_up_nsa[l].astype(BF16), w_up_fox[l].astype(BF16), w_out[l].astype(BF16),
              norm_ffn[l][None, :], w_ffn_gate[l].astype(BF16), w_ffn_up[l].astype(BF16),
              w_ffn_down[l].astype(BF16), norm_ple[l][None, :], w_ple_gate[l].astype(BF16),
              w_ple_proj[l].astype(BF16)),
    )


def _rope_tables(pos):
    half = HEAD_DIM // 2
    inv_freq = ROPE_THETA ** (-jnp.arange(half, dtype=F32) / half)
    ang = inv_freq[:, None] * pos.astype(F32)[None, :]
    return jnp.cos(ang), jnp.sin(ang)


def _prompt_consts(T):
    tpos = jnp.arange(T, dtype=jnp.int32)
    blk = tpos // CMP_BLOCK
    avg = jnp.where(jnp.arange(128)[None, :] == 64 + blk[:, None], 1.0 / CMP_BLOCK, 0.0).astype(BF16)
    onehot = (jnp.arange(64)[:, None] == blk[None, :]).astype(BF16)
    cos, sin = _rope_tables(tpos)
    return dict(avg=avg, avgT=avg.T, onehot=onehot, cos=cos, sin=sin)


def _prompt_layer(xp, p_l, L, C, tm_proj, tq_fox, tm_tail):
    B, T, D = xp.shape
    nsaT, winT, foxT, logfT, qn, qf, kf, kvb, ga, gm, _ = _proj(
        xp, L["gmix"], L["wt"], L["wg"], C["cos"], C["sin"], L["hg"], L["fb"], tm_proj)
    kcb, vcb = _cmp_prompt(nsaT, C["avg"], C["avgT"], L["peT"], L["pe"], L["w0T"], L["w1"])
    o_a = _nsa_prompt(qn, kcb, vcb, kvb, C["onehot"], ga)
    o_b = _fox_prompt(qf, kf, kvb, tq_fox)
    N = B * T
    y = _tail(xp.reshape(N, D), o_a.reshape(N, 512), o_b.reshape(N, 512), gm.reshape(N, gm.shape[-1]),
              p_l.reshape(N, p_l.shape[-1]), L["tail"], tm_tail).reshape(B, T, D)
    nsa_rows = jnp.transpose(nsaT.reshape(B, 4, NSA_KV_HEADS, HEAD_DIM, T), (0, 4, 1, 2, 3))
    fox_rows = jnp.transpose(foxT.reshape(B, 2, FOX_HEADS, HEAD_DIM, T), (0, 4, 1, 2, 3))
    logf = jnp.transpose(logfT, (0, 2, 1))
    wk = min(WINDOW, T)
    win_rows = jnp.transpose(winT[:, :, T - wk:].reshape(B, 2, NSA_KV_HEADS, HEAD_DIM, wk), (0, 4, 1, 2, 3))
    return y, nsa_rows, fox_rows, logf, win_rows


_NBUF = 6


def _lane_sum_rows(a):
    ones = jnp.ones((8, a.shape[1]), BF16)
    a1, a2, a3 = _split3(a)
    return _nt(ones, a1) + _nt(ones, a2) + _nt(ones, a3)


def _fox_sample_kernel(pt_ref, cf_ref, clf_ref, qb_ref, kn_ref, vn_ref, lfn_ref, o_ref,
                       ring, s_ref, lf_ref, acc_ref, sn_ref, sem, lsem, *, layer, n_pages):
    b = pl.program_id(0)
    NP = n_pages
    H = FOX_HEADS

    def page_copy(i, slot):
        c = i // NP
        pg = i - c * NP
        return pltpu.make_async_copy(cf_ref.at[layer, pt_ref[b, pg], c], ring.at[slot], sem.at[slot])

    def logf_copy(pg):
        return pltpu.make_async_copy(clf_ref.at[layer, pt_ref[b, pg]],
                                     lf_ref.at[:, pl.ds(pl.multiple_of(pg * 128, 128), 128)], lsem)

    def start_logf(pg, c):
        logf_copy(pg).start()
        return c

    lax.fori_loop(0, NP, start_logf, 0)
    for s in range(_NBUF - 1):
        page_copy(s, s).start()

    def advance(i):
        slot = i % _NBUF
        page_copy(i, slot).wait()
        nxt = i + _NBUF - 1

        @pl.when(nxt < 2 * NP)
        def _():
            page_copy(nxt, nxt % _NBUF).start()

        return slot

    def kbody(i, c):
        slot = advance(i)
        off = pl.multiple_of(i * 128, 128)
        for h in range(H):
            s_ref[h:h + 1, pl.ds(off, 128)] = jnp.sum(ring[slot, h] * qb_ref[0, h], axis=0, keepdims=True)
        return c

    lax.fori_loop(0, NP, kbody, 0)

    def wait_logf(pg, c):
        logf_copy(pg).wait()
        return c

    lax.fori_loop(0, NP, wait_logf, 0)

    for h in range(H):
        sn_ref[h:h + 1, :] = jnp.sum(kn_ref[0, h] * qb_ref[0, h], axis=0, keepdims=True)
    s_new = sn_ref[...]
    cn = lfn_ref[0]
    ii = lax.broadcasted_iota(jnp.int32, (128, 128), 0)
    jj = lax.broadcasted_iota(jnp.int32, (128, 128), 1)
    later = jnp.where(ii > jj, 1.0, 0.0).astype(BF16)

    def sbody(k, carry):
        tot, mx = carry
        off = pl.multiple_of((NP - 1 - k) * 128, 128)
        chunk = lf_ref[:, pl.ds(off, 128)]
        l1, l2, l3 = _split3(chunk)
        within = _dot(l1, later) + _dot(l2, later) + _dot(l3, later)
        sc = s_ref[:, pl.ds(off, 128)] + cn + within + tot
        s_ref[:, pl.ds(off, 128)] = sc
        return tot + jnp.sum(chunk, axis=1, keepdims=True), jnp.maximum(mx, jnp.max(sc, axis=1, keepdims=True))

    _, mx = lax.fori_loop(0, NP, sbody, (jnp.zeros((H, 1), F32), jnp.full((H, 1), NEG_INF, F32)))
    m = jnp.maximum(mx, s_new + cn - cn)

    def pbody(k, lsum):
        off = pl.multiple_of(k * 128, 128)
        p = jnp.exp(s_ref[:, pl.ds(off, 128)] - m)
        s_ref[:, pl.ds(off, 128)] = p
        return lsum + p

    lsum = lax.fori_loop(0, NP, pbody, jnp.zeros((H, 128), F32))
    p_new = jnp.exp(s_new + cn - cn - m)
    denom = jnp.sum(lsum, axis=1, keepdims=True) + p_new[:, 0:1]

    acc_ref[...] = jnp.zeros_like(acc_ref)

    def vbody(i, c):
        slot = advance(i)
        off = pl.multiple_of((i - NP) * 128, 128)
        for h in range(H):
            acc_ref[h] = acc_ref[h] + ring[slot, h] * s_ref[h:h + 1, pl.ds(off, 128)]
        return c

    lax.fori_loop(NP, 2 * NP, vbody, 0)

    lane = lax.broadcasted_iota(jnp.int32, (HEAD_DIM, 128), 1)
    for h in range(H):
        a = acc_ref[h] + jnp.where(lane == 0, p_new[h:h + 1, :] * vn_ref[0, h], 0.0)
        o_ref[0, h:h + 1, :] = _lane_sum_rows(a)[0:1] / denom[h:h + 1]


def _fox_sample(layer, page_table, cfT, clfT, qb, knb, vnb, lfnb):
    DB, NP = page_table.shape
    P = NP * PAGE_SIZE
    per_seq = lambda *shape: pl.BlockSpec((1,) + shape, lambda b, pt: (b,) + (0,) * len(shape))
    grid_spec = pltpu.PrefetchScalarGridSpec(
        num_scalar_prefetch=1, grid=(DB,),
        in_specs=[pl.BlockSpec(memory_space=pl.ANY), pl.BlockSpec(memory_space=pl.ANY),
                  per_seq(8, 64, 128), per_seq(8, 64, 128), per_seq(8, 64, 128), per_seq(8, 128)],
        out_specs=per_seq(8, 64),
        scratch_shapes=[pltpu.VMEM((_NBUF, 8, 64, 128), F32), pltpu.VMEM((8, P), F32),
                        pltpu.VMEM((8, P), F32), pltpu.VMEM((8, 64, 128), F32), pltpu.VMEM((8, 128), F32),
                        pltpu.SemaphoreType.DMA((_NBUF,)), pltpu.SemaphoreType.DMA(())])
    return pl.pallas_call(
        functools.partial(_fox_sample_kernel, layer=layer, n_pages=NP),
        grid_spec=grid_spec, out_shape=jax.ShapeDtypeStruct((DB, 8, 64), F32),
        compiler_params=_cparams(1), name="fox_sample",
    )(page_table, cfT, clfT, qb, knb, vnb, lfnb)


def _split2(x):
    a = x.astype(BF16)
    return a, (x - a.astype(F32)).astype(BF16)


def _nsa_sample_kernel(pt_ref, cn_ref, qb_ref, qrow_ref, newb_ref, newrow_ref, vnr_ref, swin_ref,
                       gate_ref, avg8_ref, wbk_ref, wbv_ref, pem_ref, exp_ref,
                       o_ref, nwin_ref,
                       ring, s_ref, m_ref, m8_ref, acc_ref, sn_ref, sem, *, layer, n_pages):
    b = pl.program_id(0)
    NP = n_pages
    P = NP * PAGE_SIZE
    nbp = P // CMP_BLOCK
    NBP = m_ref.shape[0]
    H = NSA_HEADS
    wb = swin_ref.shape[-1]
    kch = exp_ref.shape[1]

    def copy1(i, slot):
        return pltpu.make_async_copy(cn_ref.at[layer, pt_ref[b, i], pl.ds(0, 3)], ring.at[slot], sem.at[slot])

    def copy2(i, slot):
        return pltpu.make_async_copy(cn_ref.at[layer, pt_ref[b, i - NP], 3], ring.at[slot, 0], sem.at[slot])

    def prefetch(nxt):
        @pl.when(nxt < NP)
        def _():
            copy1(nxt, nxt % _NBUF).start()

        @pl.when((nxt >= NP) & (nxt < 2 * NP))
        def _():
            copy2(nxt, nxt % _NBUF).start()

    for s in range(_NBUF - 1):
        prefetch(jnp.int32(s))
    m_ref[...] = jnp.zeros_like(m_ref)

    def body1(i, c):
        slot = i % _NBUF
        copy1(i, slot).wait()
        prefetch(i + _NBUF - 1)
        x1, x2 = _split2(ring[slot, 0:2].reshape(256, 128))
        a = avg8_ref[i % 4]
        contrib = _nt(a, x1) + _nt(a, x2)

        @pl.when(i % 4 == 0)
        def _():
            m8_ref[...] = contrib

        @pl.when(i % 4 != 0)
        def _():
            m8_ref[...] = m8_ref[...] + contrib

        @pl.when(i % 4 == 3)
        def _():
            m_ref[pl.ds(pl.multiple_of((i // 4) * 8, 8), 8), :] = m8_ref[...]

        off = pl.multiple_of(i * 128, 128)
        for h in range(H):
            s_ref[h:h + 1, pl.ds(off, 128)] = jnp.sum(ring[slot, 2, h // NSA_GROUP] * qb_ref[0, h],
                                                      axis=0, keepdims=True)
        return c

    lax.fori_loop(0, NP, body1, 0)

    rowm = lax.broadcasted_iota(jnp.int32, (NBP, 1), 0)
    pem = pem_ref[...]
    mm = m_ref[...] + jnp.where(rowm < nbp, pem, 0.0)
    mm = jnp.where(rowm == nbp, newrow_ref[0, 0:1, :] * (1.0 / CMP_BLOCK) + pem, mm)
    mb = mm.astype(BF16)
    n8 = lax.broadcasted_iota(jnp.int32, (8, NBP), 1)
    n1 = n8[0:1]
    row8 = lax.broadcasted_iota(jnp.int32, (8, NBP), 0)
    rowo = lax.broadcasted_iota(jnp.int32, (8, 128), 0)
    cur = nbp
    o_cs, sels = [], []
    for kv in range(NSA_KV_HEADS):
        kcb = _dot(mb, wbk_ref[kv]).astype(BF16)
        vcb = _dot(mb, wbv_ref[kv]).astype(BF16)
        sc = _nt(qrow_ref[0, kv], kcb)
        valid = (n8 + 1) * CMP_BLOCK - 1 <= P
        s = jnp.where(valid, sc, NEG_INF)
        e = jnp.exp(s - jnp.max(s, axis=1, keepdims=True))
        p_c = jnp.where(valid, e / jnp.sum(e, axis=1, keepdims=True), 0.0)
        o_cs.append(_dot(p_c.astype(BF16), vcb))
        imp = p_c[0:1] + p_c[1:2] + p_c[2:3] + p_c[3:4]
        imp = jnp.where((n1 == cur) | (n1 == 0), FORCED, imp)
        v = jnp.where(n1 > cur, -2.0, imp)
        sel = jnp.zeros((1, NBP), F32)
        for _ in range(N_SELECT):
            mx = jnp.max(v, axis=1, keepdims=True)
            idx = jnp.min(jnp.where(v == mx, n1, 1 << 30), axis=1, keepdims=True)
            hit = n1 == idx
            sel = jnp.where(hit, 1.0, sel)
            v = jnp.where(hit, -3.0, v)
        sels.append(jnp.where(n1 <= cur, sel, 0.0))
    sel8 = jnp.where(row8 < NSA_GROUP, sels[0], sels[1]).astype(BF16)

    mx = jnp.full((H, 1), NEG_INF, F32)
    for ci in range(P // kch):
        mk = _dot(sel8[:, ci * 128:(ci + 1) * 128], exp_ref[...])
        sm = jnp.where(mk > 0.5, s_ref[:, ci * kch:(ci + 1) * kch], NEG_INF)
        s_ref[:, ci * kch:(ci + 1) * kch] = sm
        mx = jnp.maximum(mx, jnp.max(sm, axis=1, keepdims=True))
    for h in range(H):
        kv = h // NSA_GROUP
        sn_ref[h:h + 1, :] = jnp.sum(newb_ref[0, 256 + kv * 64:256 + (kv + 1) * 64, :] * qb_ref[0, h],
                                     axis=0, keepdims=True)
    s_new = sn_ref[...]
    m = jnp.maximum(mx, s_new)

    def pbody(k, lsum):
        off = pl.multiple_of(k * 128, 128)
        p = jnp.exp(s_ref[:, pl.ds(off, 128)] - m)
        s_ref[:, pl.ds(off, 128)] = p
        return lsum + p

    lsum = lax.fori_loop(0, NP, pbody, jnp.zeros((H, 128), F32))
    p_new = jnp.exp(s_new - m)
    denom = jnp.sum(lsum, axis=1, keepdims=True) + p_new[:, 0:1]

    acc_ref[...] = jnp.zeros_like(acc_ref)

    def body2(i, c):
        slot = i % _NBUF
        copy2(i, slot).wait()
        prefetch(i + _NBUF - 1)
        off = pl.multiple_of((i - NP) * 128, 128)
        for h in range(H):
            acc_ref[h] = acc_ref[h] + ring[slot, 0, h // NSA_GROUP] * s_ref[h:h + 1, pl.ds(off, 128)]
        return c

    lax.fori_loop(NP, 2 * NP, body2, 0)

    zpad = jnp.zeros((HEAD_DIM, 128), F32)
    lane5 = lax.broadcasted_iota(jnp.int32, (8, wb), 1)
    lanew = lax.broadcasted_iota(jnp.int32, (HEAD_DIM, wb), 1)
    zkw = jnp.zeros((HEAD_DIM, wb), BF16)
    for kv in range(NSA_KV_HEADS):
        o_s = jnp.zeros((8, 128), F32)
        for g in range(NSA_GROUP):
            h = kv * NSA_GROUP + g
            r = _lane_sum_rows(jnp.concatenate([acc_ref[h], zpad], axis=0))
            r = (r + p_new[h:h + 1, 0:1] * vnr_ref[0, kv, 0]) / denom[h:h + 1]
            o_s = jnp.where(rowo == g, r, o_s)
        qr = qrow_ref[0, kv]
        kw = swin_ref[0, 0, 0, kv]
        vw = swin_ref[0, 0, 1, kv]
        sw = _dot(qr, jnp.concatenate([kw.astype(BF16), zkw], axis=0))
        valid_w = (wb - lane5) < WINDOW
        sw = jnp.where(valid_w, sw, NEG_INF)
        for g in range(NSA_GROUP):
            h = kv * NSA_GROUP + g
            sn_ref[g:g + 1, :] = jnp.sum(newb_ref[0, 512 + kv * 64:512 + (kv + 1) * 64, :] * qb_ref[0, h],
                                         axis=0, keepdims=True)
        s_wn = sn_ref[:, 0:1]
        m_w = jnp.maximum(jnp.max(sw, axis=1, keepdims=True), s_wn)
        p_w = jnp.where(valid_w, jnp.exp(sw - m_w), 0.0)
        p_wn = jnp.exp(s_wn - m_w)
        l_w = jnp.sum(p_w, axis=1, keepdims=True) + p_wn
        o_w = (_nt(p_w.astype(BF16), jnp.concatenate([vw.astype(BF16), zkw], axis=0))
               + p_wn * vnr_ref[0, kv, 1]) / l_w
        o_ref[0, kv] = (gate_ref[0, kv, 0] * o_cs[kv] + gate_ref[0, kv, 1] * o_s
                        + gate_ref[0, kv, 2] * o_w)
        for c in range(2):
            col = newb_ref[0, 512 + c * 128 + kv * 64:512 + c * 128 + (kv + 1) * 64, 0:1]
            rolled = pltpu.roll(swin_ref[0, 0, c, kv], wb - 1, 1)
            nwin_ref[0, c, kv] = jnp.where(lanew == wb - 1, col, rolled)


def _nsa_sample(layer, page_table, cnT, qb, qrow, newb, newrow, vnr, swT, gates, CS):
    DB, NP = page_table.shape
    P = NP * PAGE_SIZE
    wb = swT.shape[-1]
    NBP = -(-(P // CMP_BLOCK + 1) // 128) * 128
    assert NP % 4 == 0
    per_seq = lambda *shape: pl.BlockSpec((1,) + shape, lambda b, pt: (b,) + (0,) * len(shape))
    const = lambda a: pl.BlockSpec(a.shape, lambda b, pt: (0,) * a.ndim)
    consts = [CS["avg8"], CS["wbk"], CS["wbv"], CS["pem"], CS["expand"]]
    grid_spec = pltpu.PrefetchScalarGridSpec(
        num_scalar_prefetch=1, grid=(DB,),
        in_specs=[pl.BlockSpec(memory_space=pl.ANY),
                  per_seq(8, 64, 128), per_seq(2, 8, 128), per_seq(768, 128), per_seq(8, 256),
                  per_seq(2, 2, 8, 128),
                  pl.BlockSpec((1, 1, 2, 2, 64, wb), lambda b, pt: (layer, b, 0, 0, 0, 0)),
                  per_seq(2, 3, 8, 128)] + [const(a) for a in consts],
        out_specs=[per_seq(2, 8, 128), per_seq(2, 2, 64, wb)],
        scratch_shapes=[pltpu.VMEM((_NBUF, 3, 2, 64, 128), F32), pltpu.VMEM((8, P), F32),
                        pltpu.VMEM((NBP, 256), F32), pltpu.VMEM((8, 256), F32),
                        pltpu.VMEM((8, 64, 128), F32), pltpu.VMEM((8, 128), F32),
                        pltpu.SemaphoreType.DMA((_NBUF,))])
    return pl.pallas_call(
        functools.partial(_nsa_sample_kernel, layer=layer, n_pages=NP),
        grid_spec=grid_spec,
        out_shape=[jax.ShapeDtypeStruct((DB, 2, 8, 128), F32),
                   jax.ShapeDtypeStruct((DB, 2, 2, 64, wb), F32)],
        compiler_params=_cparams(1), name="nsa_sample",
    )(page_table, cnT, qb, qrow, newb, newrow, vnr, swT, gates, *consts)


def _sample_consts(P, L):
    r = jnp.arange(128)
    avg8 = jnp.stack([jnp.where(jnp.arange(8)[:, None] == 2 * p + r[None, :] // CMP_BLOCK,
                                1.0 / CMP_BLOCK, 0.0) for p in range(4)]).astype(BF16)
    w = L["w_cmp"]
    z = jnp.zeros((64, 128), F32)
    wpad = lambda c: jnp.concatenate([w[c], jnp.zeros((64, 64), F32)], axis=1)
    blocks = lambda c, kv: jnp.concatenate(
        [wpad(c) if (cc, kk) == (c, kv) else z for cc in range(2) for kk in range(2)], axis=0)
    wbk = jnp.stack([blocks(0, kv) for kv in range(2)]).astype(BF16)
    wbv = jnp.stack([blocks(1, kv) for kv in range(2)]).astype(BF16)
    pm = jnp.mean(L["pe"], axis=1)
    pem = jnp.concatenate([pm[0], pm[0], pm[1], pm[1]])[None, :]
    kch = min(P, 8192)
    expand = (r[:, None] == (jnp.arange(kch) // CMP_BLOCK)[None, :]).astype(BF16)
    return dict(avg8=avg8, wbk=wbk, wbv=wbv, pem=pem, expand=expand)


def _sample_layer(layer, xs, p_l, L, cos_s, sin_s, page_table, cnT, cfT, clfT, swT):
    DB, D = xs.shape
    P = page_table.shape[1] * PAGE_SIZE
    xpad = jnp.zeros((1, 128, D), F32).at[0, :DB].set(xs)
    nsaT, winT, foxT, logfT, qn, qf, kf, kvb, ga, gm, qT = _proj(
        xpad, L["gmix"], L["wt"], L["wg"], cos_s, sin_s, L["hg"], L["fb"], 128)
    lanes = lambda a: jnp.broadcast_to(a[..., None], a.shape + (128,))
    q_all = (qT[0, :, :DB] * SCALE).T
    qb_a = lanes(q_all[:, :512].reshape(DB, 8, 64))
    qb_f = lanes(q_all[:, 512:].reshape(DB, 8, 64))
    new = jnp.concatenate([nsaT[0], winT[0]], axis=0)[:, :DB].T
    fx = foxT[0][:, :DB].T
    lf = logfT[0][:, :DB].T
    qrow = jnp.transpose(qn[0, :, :DB, :], (1, 0, 2)).reshape(DB, 2, 4, 128)
    qrow = jnp.concatenate([qrow, jnp.zeros_like(qrow)], axis=2)
    g = jnp.transpose(ga[0, :, :DB, :12].reshape(2, DB, 4, 3), (1, 0, 3, 2))
    gates = lanes(jnp.concatenate([g, jnp.zeros_like(g)], axis=3))
    vrow = jnp.stack([new[:, 384:512].reshape(DB, 2, 64), new[:, 640:768].reshape(DB, 2, 64)], axis=2)
    vrow = jnp.concatenate([vrow, jnp.zeros_like(vrow)], axis=-1)
    vnr = jnp.broadcast_to(vrow[:, :, :, None, :], (DB, 2, 2, 8, 128))
    newrow = jnp.broadcast_to(new[:, None, :256], (DB, 8, 256))
    CS = _sample_consts(P, L)
    o_a4, new_win = _nsa_sample(layer, page_table, cnT, qb_a, qrow, lanes(new), newrow, vnr, swT, gates, CS)
    o_a = o_a4[:, :, :4, :64].reshape(DB, 512)
    o_b = _fox_sample(layer, page_table, cfT, clfT, qb_f, lanes(fx[:, :512].reshape(DB, 8, 64)),
                      lanes(fx[:, 512:].reshape(DB, 8, 64)), lanes(lf)).reshape(DB, 512)
    y = _tail(xs, o_a, o_b, gm[0, :DB], p_l, L["tail"], DB)
    return (y, new[:, :512].reshape(DB, 1, 4, NSA_KV_HEADS, HEAD_DIM),
            fx.reshape(DB, 1, 2, FOX_HEADS, HEAD_DIM), lf.reshape(DB, 1, FOX_HEADS),
            jnp.transpose(new_win, (0, 4, 1, 2, 3)))


def kernel(x_prompt, x_sample, p_prompt, p_sample, cache_nsa, cache_fox, cache_fox_logf, state_win, page_table, norm_mix, w_in, q_norm_nsa, k_norm_nsa, q_norm_fox, k_norm_fox, fox_fbias, cmp_pe, w_cmp, w_up_nsa, w_up_fox, w_out, norm_ffn, w_ffn_gate, w_ffn_up, w_ffn_down, norm_ple, w_ple_gate, w_ple_proj):
    depth = w_in.shape[0]
    B, T, D = x_prompt.shape
    DB, S, _ = x_sample.shape
    assert S == 1 and DB <= 128
    P = page_table.shape[1] * PAGE_SIZE
    cnT = jnp.transpose(cache_nsa, (0, 1, 3, 4, 5, 2))
    cfT = jnp.transpose(cache_fox, (0, 1, 3, 4, 5, 2))
    clfT = jnp.transpose(cache_fox_logf, (0, 1, 3, 2))
    swT = jnp.transpose(state_win, (0, 1, 3, 4, 5, 2))
    C = _prompt_consts(T)
    cos_s, sin_s = _rope_tables(jnp.full((128,), P, jnp.int32))
    tm = min(256, T)
    xp, xs = x_prompt, x_sample.reshape(DB, D)
    outs = [[] for _ in range(8)]
    for l in range(depth):
        L = _prep_layer(l, norm_mix, w_in, q_norm_nsa, k_norm_nsa, q_norm_fox, k_norm_fox, fox_fbias,
                        cmp_pe, w_cmp, w_up_nsa, w_up_fox, w_out, norm_ffn, w_ffn_gate, w_ffn_up,
                        w_ffn_down, norm_ple, w_ple_gate, w_ple_proj)
        xp, nsa_p, fox_p, lf_p, win_p = _prompt_layer(xp, p_prompt[l], L, C, tm, tm, tm)
        xs, nsa_s, fox_s, lf_s, win_s = _sample_layer(l, xs, p_sample[l].reshape(DB, -1), L, cos_s, sin_s,
                                                      page_table, cnT, cfT, clfT, swT)
        for acc, v in zip(outs, (nsa_p, nsa_s, fox_p, fox_s, lf_p, lf_s, win_p, win_s)):
            acc.append(v)
    return (xp, xs.reshape(DB, 1, D)) + tuple(jnp.stack(o) for o in outs)
```

```python
import functools

import jax
import jax.numpy as jnp
import numpy as np
from jax import lax
from jax.experimental import pallas as pl
from jax.experimental.pallas import tpu as pltpu

HEAD_DIM = 64
NSA_HEADS = 8
NSA_KV_HEADS = 2
NSA_GROUP = NSA_HEADS // NSA_KV_HEADS
FOX_HEADS = 8
CMP_BLOCK = 64
N_SELECT = 16
WINDOW = 512
PAGE_SIZE = 128
ROPE_THETA = 10000.0
EPS = 1e-6
NEG_INF = -1e30
FORCED = 1e4
SCALE = HEAD_DIM ** -0.5

LANES = 128
VMEM_LIMIT = 56 * 1024 * 1024
UNSELECTED = -30000.0

F32 = jnp.float32
BF16 = jnp.bfloat16

_R_QA, _R_KV6, _R_QB, _R_KB, _R_VB, _R_FB, _R_END = 0, 512, 1280, 1792, 2304, 2816, 2832


def _nt(a, b):
    return lax.dot_general(a, b, (((1,), (1,)), ((), ())), preferred_element_type=F32)


def _dot(a, b):
    return jnp.dot(a, b, preferred_element_type=F32)


def _split3(x):
    a = x.astype(BF16)
    r = x - a.astype(F32)
    b = r.astype(BF16)
    c = (r - b.astype(F32)).astype(BF16)
    return a, b, c


def _log_sigmoid(x):
    return jnp.minimum(x, 0.0) - jnp.log1p(jnp.exp(-jnp.abs(x)))


def _cparams(n_grid):
    return pltpu.CompilerParams(dimension_semantics=("arbitrary",) * n_grid,
                                vmem_limit_bytes=VMEM_LIMIT)


def _proj_kernel(x_ref, gmix_ref, wt_ref, wg_ref, cos_ref, sin_ref, hg_ref, fb_ref,
                 nsaT_ref, winT_ref, foxT_ref, logfT_ref, qn_ref, qf_ref, kf_ref, kvb_ref,
                 ga_ref, gm_ref, qT_ref, carry_ref, *, tm):
    j = pl.program_id(1)

    @pl.when(j == 0)
    def _():
        carry_ref[...] = jnp.zeros_like(carry_ref)

    x = x_ref[0]
    ms = jnp.mean(x * x, axis=-1, keepdims=True)
    h = (x * lax.rsqrt(ms + EPS) * gmix_ref[...]).astype(BF16)
    cos = cos_ref[...]
    sin = sin_ref[...]
    hg = hg_ref[...]

    def zt(r0, r1):
        return _nt(wt_ref[r0:r1, :], h)

    def headnorm(z, col):
        ss = jnp.sum(z * z, axis=0, keepdims=True)
        return z * lax.rsqrt(ss * (1.0 / HEAD_DIM) + EPS) * hg[:, col:col + 1]

    def rope(z):
        x1, x2 = z[:HEAD_DIM // 2], z[HEAD_DIM // 2:]
        return jnp.concatenate([x1 * cos - x2 * sin, x2 * cos + x1 * sin], axis=0)

    zeros64 = jnp.zeros((HEAD_DIM, tm), F32)
    row8 = lax.broadcasted_iota(jnp.int32, (8, tm), 0)

    for hp in range(NSA_HEADS // 2):
        z = zt(_R_QA + hp * 128, _R_QA + (hp + 1) * 128)
        for i in range(2):
            hd = hp * 2 + i
            q = rope(headnorm(z[i * 64:(i + 1) * 64], 0))
            qT_ref[0, hd * 64:(hd + 1) * 64, :] = q
            blk = jnp.concatenate([q * SCALE, zeros64], axis=0)
            qn_ref[0, hd] = blk.T.astype(BF16)

    for part in range(6):
        z = zt(_R_KV6 + part * 128, _R_KV6 + (part + 1) * 128)
        if part % 2 == 0:
            z = jnp.concatenate([rope(headnorm(z[:64], 1 + part // 2)),
                                 rope(headnorm(z[64:], 1 + part // 2))], axis=0)
        if part < 4:
            nsaT_ref[0, part * 128:(part + 1) * 128, :] = z
        else:
            winT_ref[0, (part - 4) * 128:(part - 3) * 128, :] = z
        if part >= 2:
            kvb_ref[0, (part - 2) * 128:(part - 1) * 128, :] = z.astype(BF16)

    zf = zt(_R_FB, _R_END)[:8]
    logf = _log_sigmoid(zf + fb_ref[...])
    logfT_ref[0] = logf
    ii = lax.broadcasted_iota(jnp.int32, (tm, tm), 0)
    jj = lax.broadcasted_iota(jnp.int32, (tm, tm), 1)
    tri = jnp.where(ii <= jj, 1.0, 0.0).astype(BF16)
    l1, l2, l3 = _split3(logf)
    c = carry_ref[:, 0:1] + (_dot(l1, tri) + _dot(l2, tri) + _dot(l3, tri))
    carry_ref[...] = jnp.broadcast_to(c[:, tm - 1:tm], carry_ref.shape)
    c1, c2, c3 = [p.astype(F32) for p in _split3(c)]

    for hp in range(FOX_HEADS // 2):
        z = zt(_R_QB + hp * 128, _R_QB + (hp + 1) * 128)
        for i in range(2):
            hd = hp * 2 + i
            q = headnorm(z[i * 64:(i + 1) * 64], 4)
            qT_ref[0, 512 + hd * 64:512 + (hd + 1) * 64, :] = q
            aug = jnp.where(row8 == 0, c1[hd:hd + 1],
                            jnp.where(row8 == 1, c2[hd:hd + 1],
                                      jnp.where(row8 == 2, c3[hd:hd + 1],
                                                jnp.where(row8 < 6, 1.0, 0.0))))
            blk = jnp.concatenate([q * SCALE, aug, jnp.zeros((56, tm), F32)], axis=0)
            qf_ref[0, hd] = blk.T.astype(BF16)

    for hp in range(FOX_HEADS // 2):
        z = zt(_R_KB + hp * 128, _R_KB + (hp + 1) * 128)
        for i in range(2):
            hd = hp * 2 + i
            k = headnorm(z[i * 64:(i + 1) * 64], 5)
            foxT_ref[0, hd * 64:(hd + 1) * 64, :] = k
            aug = jnp.where(row8 < 3, 1.0,
                            jnp.where(row8 == 3, -c1[hd:hd + 1],
                                      jnp.where(row8 == 4, -c2[hd:hd + 1],
                                                jnp.where(row8 == 5, -c3[hd:hd + 1], 0.0))))
            blk = jnp.concatenate([k, aug, jnp.zeros((56, tm), F32)], axis=0)
            kf_ref[0, hd] = blk.astype(BF16)

    for hp in range(FOX_HEADS // 2):
        z = zt(_R_VB + hp * 128, _R_VB + (hp + 1) * 128)
        foxT_ref[0, 512 + hp * 128:512 + (hp + 1) * 128, :] = z
        kvb_ref[0, 512 + hp * 128:512 + (hp + 1) * 128, :] = z.astype(BF16)

    zg = _dot(h, wg_ref[:, 0:256])
    ga_ref[0, 0] = jax.nn.sigmoid(zg[:, 0:128])
    ga_ref[0, 1] = jax.nn.sigmoid(zg[:, 128:256])
    ngm = gm_ref.shape[-1]
    for c0 in range(0, ngm, 512):
        gm_ref[0, :, c0:c0 + 512] = jax.nn.sigmoid(_dot(h, wg_ref[:, 256 + c0:256 + c0 + 512]))


def _proj(x, gmix, wt, wg, cos, sin, hg, fb, tm):
    B, T, D = x.shape
    grid = (B, T // tm)
    ngm = wg.shape[1] - 256
    const = lambda *shape: pl.BlockSpec(shape, lambda b, j: (0,) * len(shape))
    out_shape = [
        jax.ShapeDtypeStruct((B, 512, T), F32),
        jax.ShapeDtypeStruct((B, 256, T), F32),
        jax.ShapeDtypeStruct((B, 1024, T), F32),
        jax.ShapeDtypeStruct((B, 8, T), F32),
        jax.ShapeDtypeStruct((B, NSA_HEADS, T, 128), BF16),
        jax.ShapeDtypeStruct((B, FOX_HEADS, T, 128), BF16),
        jax.ShapeDtypeStruct((B, FOX_HEADS, 128, T), BF16),
        jax.ShapeDtypeStruct((B, 1024, T), BF16),
        jax.ShapeDtypeStruct((B, 2, T, 128), F32),
        jax.ShapeDtypeStruct((B, T, ngm), F32),
        jax.ShapeDtypeStruct((B, 1024, T), F32),
    ]
    out_specs = [
        pl.BlockSpec((1, 512, tm), lambda b, j: (b, 0, j)),
        pl.BlockSpec((1, 256, tm), lambda b, j: (b, 0, j)),
        pl.BlockSpec((1, 1024, tm), lambda b, j: (b, 0, j)),
        pl.BlockSpec((1, 8, tm), lambda b, j: (b, 0, j)),
        pl.BlockSpec((1, NSA_HEADS, tm, 128), lambda b, j: (b, 0, j, 0)),
        pl.BlockSpec((1, FOX_HEADS, tm, 128), lambda b, j: (b, 0, j, 0)),
        pl.BlockSpec((1, FOX_HEADS, 128, tm), lambda b, j: (b, 0, 0, j)),
        pl.BlockSpec((1, 1024, tm), lambda b, j: (b, 0, j)),
        pl.BlockSpec((1, 2, tm, 128), lambda b, j: (b, 0, j, 0)),
        pl.BlockSpec((1, tm, ngm), lambda b, j: (b, j, 0)),
        pl.BlockSpec((1, 1024, tm), lambda b, j: (b, 0, j)),
    ]
    in_specs = [
        pl.BlockSpec((1, tm, D), lambda b, j: (b, j, 0)),
        const(1, D),
        const(*wt.shape),
        const(*wg.shape),
        pl.BlockSpec((32, tm), lambda b, j: (0, j)),
        pl.BlockSpec((32, tm), lambda b, j: (0, j)),
        const(64, 8),
        const(8, 1),
    ]
    return pl.pallas_call(
        functools.partial(_proj_kernel, tm=tm),
        grid=grid, in_specs=in_specs, out_specs=out_specs, out_shape=out_shape,
        scratch_shapes=[pltpu.VMEM((8, 128), F32)],
        compiler_params=_cparams(2), name="proj",
    )(x, gmix, wt, wg, cos, sin, hg, fb)


def _fox_kernel(q_ref, k_ref, v_ref, o_ref, *, tq, tk):
    j = pl.program_id(2)
    row = lax.broadcasted_iota(jnp.int32, (128, tk), 0)
    lane_o = lax.broadcasted_iota(jnp.int32, (tq, 128), 1)
    one_row = (64, 0)
    keep = [jnp.where((row < 64) if i == 0 else (row >= 64), 1.0, 0.0).astype(BF16) for i in range(2)]
    fill = [jnp.where(row == one_row[i], 1.0, 0.0).astype(BF16) for i in range(2)]
    qs = [q_ref[0, i] for i in range(2)]

    def step(c, carry, causal):
        off = pl.multiple_of(c * tk, tk)
        vs = v_ref[0, :, pl.ds(off, tk)]
        out = []
        for i in range(2):
            m, acc = carry[i]
            s = _dot(qs[i], k_ref[0, i, :, pl.ds(off, tk)])
            if causal:
                rr = j * tq + lax.broadcasted_iota(jnp.int32, (tq, tk), 0)
                cc = c * tk + lax.broadcasted_iota(jnp.int32, (tq, tk), 1)
                s = jnp.where(cc <= rr, s, NEG_INF)
            m_new = jnp.maximum(m, jnp.max(s, axis=1, keepdims=True))
            p = jnp.exp(s - m_new).astype(BF16)
            acc = jnp.exp(m - m_new) * acc + _nt(p, vs * keep[i] + fill[i])
            out.append((m_new, acc))
        return tuple(out)

    init = tuple((jnp.full((tq, 1), NEG_INF, F32), jnp.zeros((tq, 128), F32)) for _ in range(2))
    last = (j * tq) // tk
    carry = lax.fori_loop(0, last, functools.partial(step, causal=False), init)
    (_, a0), (_, a1) = step(last, carry, True)
    o_ref[0] = jnp.where(lane_o < 64, a0 / a0[:, 64:65], a1 / a1[:, 0:1])


def _fox_prompt(qf, kf, kvb, tq):
    B, H, T, _ = qf.shape
    grid = (B, H // 2, T // tq)
    tk = min(512, T)
    assert T % tk == 0 and tk % tq == 0
    return pl.pallas_call(
        functools.partial(_fox_kernel, tq=tq, tk=tk),
        grid=grid,
        in_specs=[
            pl.BlockSpec((1, 2, tq, 128), lambda b, hp, j: (b, hp, j, 0)),
            pl.BlockSpec((1, 2, 128, T), lambda b, hp, j: (b, hp, 0, 0)),
            pl.BlockSpec((1, 128, T), lambda b, hp, j: (b, 4 + hp, 0)),
        ],
        out_specs=pl.BlockSpec((1, tq, 128), lambda b, hp, j: (b, j, hp)),
        out_shape=jax.ShapeDtypeStruct((B, T, 512), F32),
        compiler_params=_cparams(3), name="fox_prompt",
    )(qf, kf, kvb)


def _cmp_kernel(xk_ref, xv_ref, avg_ref, avgT_ref, peT_ref, pe_ref, w0T_ref, w1_ref,
                kcb_ref, vcb_ref):
    lane = lax.broadcasted_iota(jnp.int32, (128, 128), 1)
    rowi = lax.broadcasted_iota(jnp.int32, (128, 128), 0)
    xk = xk_ref[0]
    k1, k2, _ = _split3(xk)
    mk = _dot(k1, avg_ref[...]) + _dot(k2, avg_ref[...])
    mk = mk + jnp.mean(peT_ref[0], axis=1, keepdims=True)
    kcb = _dot(w0T_ref[...], mk.astype(BF16))
    kcb_ref[0, 0] = jnp.where(lane >= 64, kcb, 0.0).astype(BF16)
    xv = xv_ref[0]
    v1, v2, _ = _split3(xv)
    mv = _nt(avgT_ref[...], v1) + _nt(avgT_ref[...], v2)
    mv = mv + jnp.mean(pe_ref[1], axis=0, keepdims=True)
    vcb = _dot(mv.astype(BF16), w1_ref[...])
    vcb_ref[0, 0] = jnp.where(rowi >= 64, vcb, 0.0).astype(BF16)


def _cmp_prompt(nsaT, avg, avgT, peT, pe, w0T, w1):
    B, _, T = nsaT.shape
    const = lambda *shape: pl.BlockSpec(shape, lambda b, kv: (0,) * len(shape))
    return pl.pallas_call(
        _cmp_kernel,
        grid=(B, NSA_KV_HEADS),
        in_specs=[
            pl.BlockSpec((1, 64, T), lambda b, kv: (b, kv, 0)),
            pl.BlockSpec((1, 64, T), lambda b, kv: (b, 2 + kv, 0)),
            const(T, 128), const(128, T), const(2, 64, 64), const(2, 64, 64),
            const(128, 64), const(64, 128),
        ],
        out_specs=[pl.BlockSpec((1, 1, 128, 128), lambda b, kv: (b, kv, 0, 0))] * 2,
        out_shape=[jax.ShapeDtypeStruct((B, NSA_KV_HEADS, 128, 128), BF16)] * 2,
        compiler_params=_cparams(2), name="cmp_prompt",
    )(nsaT, nsaT, avg, avgT, peT, pe, w0T, w1)


def _nsa_kernel(q_ref, kcb_ref, vcb_ref, ks_ref, vs_ref, kw_ref, vw_ref, oh_ref, ga_ref,
                o_ref, *, nb, ch, ww, n_tiles):
    j = pl.program_id(2)
    QB = 128
    R = NSA_GROUP * QB
    q4 = q_ref[0].reshape(R, 128)
    lane = lax.broadcasted_iota(jnp.int32, (QB, 128), 1)
    n_idx = lane - 64
    r_q = lax.broadcasted_iota(jnp.int32, (QB, 128), 0)
    t = j * QB + r_q
    real = (lane >= 64) & (n_idx < nb)
    rep4 = lambda a: jnp.concatenate([a] * NSA_GROUP, axis=0)
    lane4 = lax.broadcasted_iota(jnp.int32, (R, 128), 1)
    t4 = j * QB + lax.broadcasted_iota(jnp.int32, (R, 128), 0) % QB

    s_c = _dot(q4, kcb_ref[0, 0])
    valid_c = (lane4 >= 64) & (lane4 - 64 < nb) & ((lane4 - 63) * CMP_BLOCK - 1 <= t4)
    s = jnp.where(valid_c, s_c, NEG_INF)
    e = jnp.exp(s - jnp.max(s, axis=1, keepdims=True))
    p_c = jnp.where(valid_c, e / jnp.sum(e, axis=1, keepdims=True), 0.0)
    o_c = _dot(p_c.astype(BF16), vcb_ref[0, 0])

    imp = p_c[0:QB] + p_c[QB:2 * QB] + p_c[2 * QB:3 * QB] + p_c[3 * QB:4 * QB]
    cur = t // CMP_BLOCK
    imp = jnp.where((n_idx == cur) | (n_idx == 0), FORCED, jnp.where(n_idx > cur, -1.0, imp))
    imp = jnp.where(real, imp, -2.0)
    impT = imp.T[64:128]
    n_row = lax.broadcasted_iota(jnp.int32, (64, QB), 0)
    curT = (j * QB + lax.broadcasted_iota(jnp.int32, (64, QB), 1)) // CMP_BLOCK
    rankT = jnp.zeros((64, QB), jnp.int32)
    for mm in range(nb):
        rowm = impT[mm:mm + 1, :]
        beats = (rowm > impT) | ((rowm == impT) & (n_row > mm))
        rankT = rankT + beats.astype(jnp.int32)
    selT = (rankT < N_SELECT) & (n_row <= curT) & (n_row < nb)
    biasT = jnp.where(selT, 0.0, UNSELECTED)
    bias = jnp.concatenate([jnp.zeros((64, QB), F32), biasT], axis=0).T
    q_sel = (q4.astype(F32) + rep4(bias)).astype(BF16)

    def ones_rows(width):
        return jnp.where(lax.broadcasted_iota(jnp.int32, (64, width), 0) == 0, 1.0, 0.0).astype(BF16)

    def attend(q, k_ref_, v_ref_, off, width, bias_q, carry):
        m, acc = carry
        k_aug = jnp.concatenate([k_ref_[0, :, pl.ds(off, width)], oh_ref[:, pl.ds(off, width)]], axis=0)
        s = _dot(q, k_aug)
        if bias_q is not None:
            s = s + rep4(bias_q)
        m_new = jnp.maximum(m, jnp.max(s, axis=1, keepdims=True))
        p = jnp.exp(s - m_new).astype(BF16)
        v_aug = jnp.concatenate([v_ref_[0, :, pl.ds(off, width)], ones_rows(width)], axis=0)
        acc = jnp.exp(m - m_new) * acc + _nt(p, v_aug)
        return m_new, acc

    init = (jnp.full((R, 1), NEG_INF, F32), jnp.zeros((R, 128), F32))

    last = (j * QB) // ch
    carry = lax.fori_loop(
        0, last, lambda c, cr: attend(q_sel, ks_ref, vs_ref, pl.multiple_of(c * ch, ch), ch, None, cr), init)
    tq_c = j * QB + lax.broadcasted_iota(jnp.int32, (QB, ch), 0)
    kp_c = last * ch + lax.broadcasted_iota(jnp.int32, (QB, ch), 1)
    _, acc_s = attend(q_sel, ks_ref, vs_ref, pl.multiple_of(last * ch, ch), ch,
                      jnp.where(kp_c <= tq_c, 0.0, NEG_INF), carry)
    o_s = acc_s / acc_s[:, 64:65]

    ws = jnp.clip(j - WINDOW // QB, 0, n_tiles - ww // QB)
    dpos = (j - ws) * QB + lax.broadcasted_iota(jnp.int32, (QB, ww), 0) \
        - lax.broadcasted_iota(jnp.int32, (QB, ww), 1)
    _, acc_w = attend(q4, kw_ref, vw_ref, pl.multiple_of(ws * QB, QB), ww,
                      jnp.where((dpos >= 0) & (dpos < WINDOW), 0.0, NEG_INF), init)
    o_w = acc_w / acc_w[:, 64:65]

    ga = ga_ref[0, 0]
    outs = []
    for g in range(NSA_GROUP):
        sl = slice(g * QB, (g + 1) * QB)
        outs.append(ga[:, 3 * g:3 * g + 1] * o_c[sl] + ga[:, 3 * g + 1:3 * g + 2] * o_s[sl]
                    + ga[:, 3 * g + 2:3 * g + 3] * o_w[sl])
    lo = lane < 64
    o_ref[0, :, 0:128] = jnp.where(lo, outs[0], pltpu.roll(outs[1], 64, 1))
    o_ref[0, :, 128:256] = jnp.where(lo, outs[2], pltpu.roll(outs[3], 64, 1))


def _nsa_prompt(qn, kcb, vcb, kvb, onehot, ga):
    B, H, T, _ = qn.shape
    nb = T // CMP_BLOCK
    assert nb <= 64
    QB = 128
    grid = (B, NSA_KV_HEADS, T // QB)
    kvspec = lambda base: pl.BlockSpec((1, 64, T), lambda b, kv, j: (b, base + kv, 0))
    return pl.pallas_call(
        functools.partial(_nsa_kernel, nb=nb, ch=min(512, T), ww=min(WINDOW + QB, T), n_tiles=T // QB),
        grid=grid,
        in_specs=[
            pl.BlockSpec((1, NSA_GROUP, QB, 128), lambda b, kv, j: (b, kv, j, 0)),
            pl.BlockSpec((1, 1, 128, 128), lambda b, kv, j: (b, kv, 0, 0)),
            pl.BlockSpec((1, 1, 128, 128), lambda b, kv, j: (b, kv, 0, 0)),
            kvspec(0), kvspec(2), kvspec(4), kvspec(6),
            pl.BlockSpec((64, T), lambda b, kv, j: (0, 0)),
            pl.BlockSpec((1, 1, QB, 128), lambda b, kv, j: (b, kv, j, 0)),
        ],
        out_specs=pl.BlockSpec((1, QB, 256), lambda b, kv, j: (b, j, kv)),
        out_shape=jax.ShapeDtypeStruct((B, T, 512), F32),
        compiler_params=_cparams(3), name="nsa_prompt",
    )(qn, kcb, vcb, kvb, kvb, kvb, kvb, onehot, ga)


def _tail_kernel(x_ref, oa_ref, ob_ref, gm_ref, p_ref, wua_ref, wub_ref, wo_ref, nf_ref,
                 wg_ref, wu_ref, wd_ref, npl_ref, wpg_ref, wpp_ref, y_ref):
    D = x_ref.shape[-1]
    u_a = _dot(oa_ref[...].astype(BF16), wua_ref[...])
    u_b = _dot(ob_ref[...].astype(BF16), wub_ref[...])
    mixed = gm_ref[:, 0:D] * u_a + gm_ref[:, D:2 * D] * u_b
    x1 = x_ref[...] + _dot(mixed.astype(BF16), wo_ref[...])

    def rms(v, g):
        return (v * lax.rsqrt(jnp.mean(v * v, axis=-1, keepdims=True) + EPS) * g).astype(BF16)

    h = rms(x1, nf_ref[...])
    ff = jax.nn.silu(_dot(h, wg_ref[...])) * _dot(h, wu_ref[...])
    x2 = x1 + _dot(ff.astype(BF16), wd_ref[...])
    gate = jax.nn.sigmoid(_dot(rms(x2, npl_ref[...]), wpg_ref[...]))
    y_ref[...] = x2 + gate * _dot(p_ref[...].astype(BF16), wpp_ref[...])


def _tail(x, oa, ob, gm, p, w, tm):
    N, D = x.shape
    tok = lambda width: pl.BlockSpec((tm, width), lambda i: (i, 0))
    const = lambda a: pl.BlockSpec(a.shape, lambda i: (0,) * a.ndim, pipeline_mode=pl.Buffered(1))
    return pl.pallas_call(
        _tail_kernel,
        grid=(N // tm,),
        in_specs=[tok(D), tok(oa.shape[1]), tok(ob.shape[1]), tok(gm.shape[1]), tok(p.shape[1])]
        + [const(a) for a in w],
        out_specs=tok(D),
        out_shape=jax.ShapeDtypeStruct((N, D), F32),
        compiler_params=_cparams(1), name="tail",
    )(x, oa, ob, gm, p, *w)


def _prep_layer(l, norm_mix, w_in, q_norm_nsa, k_norm_nsa, q_norm_fox, k_norm_fox, fox_fbias,
                cmp_pe, w_cmp, w_up_nsa, w_up_fox, w_out, norm_ffn, w_ffn_gate, w_ffn_up,
                w_ffn_down, norm_ple, w_ple_gate, w_ple_proj):
    wl = w_in[l]
    D = wl.shape[0]
    o_ga = 512 + 6 * 128
    o_qb = o_ga + 3 * NSA_HEADS
    o_fb = o_qb + 3 * 512
    o_gm = o_fb + FOX_HEADS
    wt = jnp.concatenate([wl[:, 0:o_ga], wl[:, o_qb:o_gm], jnp.zeros((D, 8), F32)], axis=1).T.astype(BF16)
    zpad = jnp.zeros((D, 128 - 12), F32)
    wg = jnp.concatenate([wl[:, o_ga:o_ga + 12], zpad, wl[:, o_ga + 12:o_ga + 24], zpad, wl[:, o_gm:]],
                         axis=1).astype(BF16)
    hg = jnp.stack([q_norm_nsa[l], k_norm_nsa[l, 0], k_norm_nsa[l, 1], k_norm_nsa[l, 2],
                    q_norm_fox[l], k_norm_fox[l], jnp.zeros((64,), F32), jnp.zeros((64,), F32)], axis=1)
    z64 = jnp.zeros((64, 64), F32)
    return dict(
        gmix=norm_mix[l][None, :], wt=wt, wg=wg, hg=hg, fb=fox_fbias[l][:, None],
        pe=cmp_pe[l], peT=jnp.swapaxes(cmp_pe[l], 1, 2),
        w0T=jnp.concatenate([w_cmp[l, 0].T, z64], axis=0).astype(BF16),
        w1=jnp.concatenate([w_cmp[l, 1], z64], axis=1).astype(BF16),
        w_cmp=w_cmp[l],
        tail=(w_up_nsa[l].astype(BF16), w_up_fox[l].astype(BF16), w_out[l].astype(BF16),
              norm_ffn[l][None, :], w_ffn_gate[l].astype(BF16), w_ffn_up[l].astype(BF16),
              w_ffn_down[l].astype(BF16), norm_ple[l][None, :], w_ple_gate[l].astype(BF16),
              w_ple_proj[l].astype(BF16)),
    )


def _rope_tables(pos):
    half = HEAD_DIM // 2
    inv_freq = ROPE_THETA ** (-jnp.arange(half, dtype=F32) / half)
    ang = inv_freq[:, None] * pos.astype(F32)[None, :]
    return jnp.cos(ang), jnp.sin(ang)


def _prompt_consts(T):
    tpos = jnp.arange(T, dtype=jnp.int32)
    blk = tpos // CMP_BLOCK
    avg = jnp.where(jnp.arange(128)[None, :] == 64 + blk[:, None], 1.0 / CMP_BLOCK, 0.0).astype(BF16)
    onehot = (jnp.arange(64)[:, None] == blk[None, :]).astype(BF16)
    cos, sin = _rope_tables(tpos)
    return dict(avg=avg, avgT=avg.T, onehot=onehot, cos=cos, sin=sin)


def _prompt_layer(xp, p_l, L, C, tm_proj, tq_fox, tm_tail):
    B, T, D = xp.shape
    nsaT, winT, foxT, logfT, qn, qf, kf, kvb, ga, gm, _ = _proj(
        xp, L["gmix"], L["wt"], L["wg"], C["cos"], C["sin"], L["hg"], L["fb"], tm_proj)
    kcb, vcb = _cmp_prompt(nsaT, C["avg"], C["avgT"], L["peT"], L["pe"], L["w0T"], L["w1"])
    o_a = _nsa_prompt(qn, kcb, vcb, kvb, C["onehot"], ga)
    o_b = _fox_prompt(qf, kf, kvb, tq_fox)
    N = B * T
    y = _tail(xp.reshape(N, D), o_a.reshape(N, 512), o_b.reshape(N, 512), gm.reshape(N, gm.shape[-1]),
              p_l.reshape(N, p_l.shape[-1]), L["tail"], tm_tail).reshape(B, T, D)
    nsa_rows = jnp.transpose(nsaT.reshape(B, 4, NSA_KV_HEADS, HEAD_DIM, T), (0, 4, 1, 2, 3))
    fox_rows = jnp.transpose(foxT.reshape(B, 2, FOX_HEADS, HEAD_DIM, T), (0, 4, 1, 2, 3))
    logf = jnp.transpose(logfT, (0, 2, 1))
    wk = min(WINDOW, T)
    win_rows = jnp.transpose(winT[:, :, T - wk:].reshape(B, 2, NSA_KV_HEADS, HEAD_DIM, wk), (0, 4, 1, 2, 3))
    return y, nsa_rows, fox_rows, logf, win_rows


_NBUF = 6


def _lane_sum_rows(a):
    ones = jnp.ones((8, a.shape[1]), BF16)
    a1, a2, a3 = _split3(a)
    return _nt(ones, a1) + _nt(ones, a2) + _nt(ones, a3)


def _fox_sample_kernel(pt_ref, cf_ref, clf_ref, qb_ref, kn_ref, vn_ref, lfn_ref, o_ref,
                       ring, s_ref, lf_ref, acc_ref, sn_ref, sem, lsem, *, layer, n_pages):
    b = pl.program_id(0)
    NP = n_pages
    H = FOX_HEADS

    def page_copy(i, slot):
        c = i // NP
        pg = i - c * NP
        return pltpu.make_async_copy(cf_ref.at[layer, pt_ref[b, pg], c], ring.at[slot], sem.at[slot])

    def logf_copy(pg):
        return pltpu.make_async_copy(clf_ref.at[layer, pt_ref[b, pg]],
                                     lf_ref.at[:, pl.ds(pl.multiple_of(pg * 128, 128), 128)], lsem)

    def start_logf(pg, c):
        logf_copy(pg).start()
        return c

    lax.fori_loop(0, NP, start_logf, 0)
    for s in range(_NBUF - 1):
        page_copy(s, s).start()

    def advance(i):
        slot = i % _NBUF
        page_copy(i, slot).wait()
        nxt = i + _NBUF - 1

        @pl.when(nxt < 2 * NP)
        def _():
            page_copy(nxt, nxt % _NBUF).start()

        return slot

    def kbody(i, c):
        slot = advance(i)
        off = pl.multiple_of(i * 128, 128)
        for h in range(H):
            s_ref[h:h + 1, pl.ds(off, 128)] = jnp.sum(ring[slot, h] * qb_ref[0, h], axis=0, keepdims=True)
        return c

    lax.fori_loop(0, NP, kbody, 0)

    def wait_logf(pg, c):
        logf_copy(pg).wait()
        return c

    lax.fori_loop(0, NP, wait_logf, 0)

    for h in range(H):
        sn_ref[h:h + 1, :] = jnp.sum(kn_ref[0, h] * qb_ref[0, h], axis=0, keepdims=True)
    s_new = sn_ref[...]
    cn = lfn_ref[0]
    ii = lax.broadcasted_iota(jnp.int32, (128, 128), 0)
    jj = lax.broadcasted_iota(jnp.int32, (128, 128), 1)
    later = jnp.where(ii > jj, 1.0, 0.0).astype(BF16)

    def sbody(k, carry):
        tot, mx = carry
        off = pl.multiple_of((NP - 1 - k) * 128, 128)
        chunk = lf_ref[:, pl.ds(off, 128)]
        l1, l2, l3 = _split3(chunk)
        within = _dot(l1, later) + _dot(l2, later) + _dot(l3, later)
        sc = s_ref[:, pl.ds(off, 128)] + cn + within + tot
        s_ref[:, pl.ds(off, 128)] = sc
        return tot + jnp.sum(chunk, axis=1, keepdims=True), jnp.maximum(mx, jnp.max(sc, axis=1, keepdims=True))

    _, mx = lax.fori_loop(0, NP, sbody, (jnp.zeros((H, 1), F32), jnp.full((H, 1), NEG_INF, F32)))
    m = jnp.maximum(mx, s_new + cn - cn)

    def pbody(k, lsum):
        off = pl.multiple_of(k * 128, 128)
        p = jnp.exp(s_ref[:, pl.ds(off, 128)] - m)
        s_ref[:, pl.ds(off, 128)] = p
        return lsum + p

    lsum = lax.fori_loop(0, NP, pbody, jnp.zeros((H, 128), F32))
    p_new = jnp.exp(s_new + cn - cn - m)
    denom = jnp.sum(lsum, axis=1, keepdims=True) + p_new[:, 0:1]

    acc_ref[...] = jnp.zeros_like(acc_ref)

    def vbody(i, c):
        slot = advance(i)
        off = pl.multiple_of((i - NP) * 128, 128)
        for h in range(H):
            acc_ref[h] = acc_ref[h] + ring[slot, h] * s_ref[h:h + 1, pl.ds(off, 128)]
        return c

    lax.fori_loop(NP, 2 * NP, vbody, 0)

    lane = lax.broadcasted_iota(jnp.int32, (HEAD_DIM, 128), 1)
    for h in range(H):
        a = acc_ref[h] + jnp.where(lane == 0, p_new[h:h + 1, :] * vn_ref[0, h], 0.0)
        o_ref[0, h:h + 1, :] = _lane_sum_rows(a)[0:1] / denom[h:h + 1]


def _fox_sample(layer, page_table, cfT, clfT, qb, knb, vnb, lfnb):
    DB, NP = page_table.shape
    P = NP * PAGE_SIZE
    per_seq = lambda *shape: pl.BlockSpec((1,) + shape, lambda b, pt: (b,) + (0,) * len(shape))
    grid_spec = pltpu.PrefetchScalarGridSpec(
        num_scalar_prefetch=1, grid=(DB,),
        in_specs=[pl.BlockSpec(memory_space=pl.ANY), pl.BlockSpec(memory_space=pl.ANY),
                  per_seq(8, 64, 128), per_seq(8, 64, 128), per_seq(8, 64, 128), per_seq(8, 128)],
        out_specs=per_seq(8, 64),
        scratch_shapes=[pltpu.VMEM((_NBUF, 8, 64, 128), F32), pltpu.VMEM((8, P), F32),
                        pltpu.VMEM((8, P), F32), pltpu.VMEM((8, 64, 128), F32), pltpu.VMEM((8, 128), F32),
                        pltpu.SemaphoreType.DMA((_NBUF,)), pltpu.SemaphoreType.DMA(())])
    return pl.pallas_call(
        functools.partial(_fox_sample_kernel, layer=layer, n_pages=NP),
        grid_spec=grid_spec, out_shape=jax.ShapeDtypeStruct((DB, 8, 64), F32),
        compiler_params=_cparams(1), name="fox_sample",
    )(page_table, cfT, clfT, qb, knb, vnb, lfnb)


def _split2(x):
    a = x.astype(BF16)
    return a, (x - a.astype(F32)).astype(BF16)


def _nsa_sample_kernel(pt_ref, cn_ref, qb_ref, qrow_ref, newb_ref, newrow_ref, vnr_ref, swin_ref,
                       gate_ref, avg8_ref, wbk_ref, wbv_ref, pem_ref, exp_ref,
                       o_ref, nwin_ref,
                       ring, s_ref, m_ref, m8_ref, acc_ref, sn_ref, sem, *, layer, n_pages):
    b = pl.program_id(0)
    NP = n_pages
    P = NP * PAGE_SIZE
    nbp = P // CMP_BLOCK
    NBP = m_ref.shape[0]
    H = NSA_HEADS
    wb = swin_ref.shape[-1]
    kch = exp_ref.shape[1]

    def copy1(i, slot):
        return pltpu.make_async_copy(cn_ref.at[layer, pt_ref[b, i], pl.ds(0, 3)], ring.at[slot], sem.at[slot])

    def copy2(i, slot):
        return pltpu.make_async_copy(cn_ref.at[layer, pt_ref[b, i - NP], 3], ring.at[slot, 0], sem.at[slot])

    def prefetch(nxt):
        @pl.when(nxt < NP)
        def _():
            copy1(nxt, nxt % _NBUF).start()

        @pl.when((nxt >= NP) & (nxt < 2 * NP))
        def _():
            copy2(nxt, nxt % _NBUF).start()

    for s in range(_NBUF - 1):
        prefetch(jnp.int32(s))
    m_ref[...] = jnp.zeros_like(m_ref)

    def body1(i, c):
        slot = i % _NBUF
        copy1(i, slot).wait()
        prefetch(i + _NBUF - 1)
        x1, x2 = _split2(ring[slot, 0:2].reshape(256, 128))
        a = avg8_ref[i % 4]
        contrib = _nt(a, x1) + _nt(a, x2)

        @pl.when(i % 4 == 0)
        def _():
            m8_ref[...] = contrib

        @pl.when(i % 4 != 0)
        def _():
            m8_ref[...] = m8_ref[...] + contrib

        @pl.when(i % 4 == 3)
        def _():
            m_ref[pl.ds(pl.multiple_of((i // 4) * 8, 8), 8), :] = m8_ref[...]

        off = pl.multiple_of(i * 128, 128)
        for h in range(H):
            s_ref[h:h + 1, pl.ds(off, 128)] = jnp.sum(ring[slot, 2, h // NSA_GROUP] * qb_ref[0, h],
                                                      axis=0, keepdims=True)
        return c

    lax.fori_loop(0, NP, body1, 0)

    rowm = lax.broadcasted_iota(jnp.int32, (NBP, 1), 0)
    pem = pem_ref[...]
    mm = m_ref[...] + jnp.where(rowm < nbp, pem, 0.0)
    mm = jnp.where(rowm == nbp, newrow_ref[0, 0:1, :] * (1.0 / CMP_BLOCK) + pem, mm)
    mb = mm.astype(BF16)
    n8 = lax.broadcasted_iota(jnp.int32, (8, NBP), 1)
    n1 = n8[0:1]
    row8 = lax.broadcasted_iota(jnp.int32, (8, NBP), 0)
    rowo = lax.broadcasted_iota(jnp.int32, (8, 128), 0)
    cur = nbp
    o_cs, sels = [], []
    for kv in range(NSA_KV_HEADS):
        kcb = _dot(mb, wbk_ref[kv]).astype(BF16)
        vcb = _dot(mb, wbv_ref[kv]).astype(BF16)
        sc = _nt(qrow_ref[0, kv], kcb)
        valid = (n8 + 1) * CMP_BLOCK - 1 <= P
        s = jnp.where(valid, sc, NEG_INF)
        e = jnp.exp(s - jnp.max(s, axis=1, keepdims=True))
        p_c = jnp.where(valid, e / jnp.sum(e, axis=1, keepdims=True), 0.0)
        o_cs.append(_dot(p_c.astype(BF16), vcb))
        imp = p_c[0:1] + p_c[1:2] + p_c[2:3] + p_c[3:4]
        imp = jnp.where((n1 == cur) | (n1 == 0), FORCED, imp)
        v = jnp.where(n1 > cur, -2.0, imp)
        sel = jnp.zeros((1, NBP), F32)
        for _ in range(N_SELECT):
            mx = jnp.max(v, axis=1, keepdims=True)
            idx = jnp.min(jnp.where(v == mx, n1, 1 << 30), axis=1, keepdims=True)
            hit = n1 == idx
            sel = jnp.where(hit, 1.0, sel)
            v = jnp.where(hit, -3.0, v)
        sels.append(jnp.where(n1 <= cur, sel, 0.0))
    sel8 = jnp.where(row8 < NSA_GROUP, sels[0], sels[1]).astype(BF16)

    mx = jnp.full((H, 1), NEG_INF, F32)
    for ci in range(P // kch):
        mk = _dot(sel8[:, ci * 128:(ci + 1) * 128], exp_ref[...])
        sm = jnp.where(mk > 0.5, s_ref[:, ci * kch:(ci + 1) * kch], NEG_INF)
        s_ref[:, ci * kch:(ci + 1) * kch] = sm
        mx = jnp.maximum(mx, jnp.max(sm, axis=1, keepdims=True))
    for h in range(H):
        kv = h // NSA_GROUP
        sn_ref[h:h + 1, :] = jnp.sum(newb_ref[0, 256 + kv * 64:256 + (kv + 1) * 64, :] * qb_ref[0, h],
                                     axis=0, keepdims=True)
    s_new = sn_ref[...]
    m = jnp.maximum(mx, s_new)

    def pbody(k, lsum):
        off = pl.multiple_of(k * 128, 128)
        p = jnp.exp(s_ref[:, pl.ds(off, 128)] - m)
        s_ref[:, pl.ds(off, 128)] = p
        return lsum + p

    lsum = lax.fori_loop(0, NP, pbody, jnp.zeros((H, 128), F32))
    p_new = jnp.exp(s_new - m)
    denom = jnp.sum(lsum, axis=1, keepdims=True) + p_new[:, 0:1]

    acc_ref[...] = jnp.zeros_like(acc_ref)

    def body2(i, c):
        slot = i % _NBUF
        copy2(i, slot).wait()
        prefetch(i + _NBUF - 1)
        off = pl.multiple_of((i - NP) * 128, 128)
        for h in range(H):
            acc_ref[h] = acc_ref[h] + ring[slot, 0, h // NSA_GROUP] * s_ref[h:h + 1, pl.ds(off, 128)]
        return c

    lax.fori_loop(NP, 2 * NP, body2, 0)

    zpad = jnp.zeros((HEAD_DIM, 128), F32)
    lane5 = lax.broadcasted_iota(jnp.int32, (8, wb), 1)
    lanew = lax.broadcasted_iota(jnp.int32, (HEAD_DIM, wb), 1)
    zkw = jnp.zeros((HEAD_DIM, wb), BF16)
    for kv in range(NSA_KV_HEADS):
        o_s = jnp.zeros((8, 128), F32)
        for g in range(NSA_GROUP):
            h = kv * NSA_GROUP + g
            r = _lane_sum_rows(jnp.concatenate([acc_ref[h], zpad], axis=0))
            r = (r + p_new[h:h + 1, 0:1] * vnr_ref[0, kv, 0]) / denom[h:h + 1]
            o_s = jnp.where(rowo == g, r, o_s)
        qr = qrow_ref[0, kv]
        kw = swin_ref[0, 0, 0, kv]
        vw = swin_ref[0, 0, 1, kv]
        sw = _dot(qr, jnp.concatenate([kw.astype(BF16), zkw], axis=0))
        valid_w = (wb - lane5) < WINDOW
        sw = jnp.where(valid_w, sw, NEG_INF)
        for g in range(NSA_GROUP):
            h = kv * NSA_GROUP + g
            sn_ref[g:g + 1, :] = jnp.sum(newb_ref[0, 512 + kv * 64:512 + (kv + 1) * 64, :] * qb_ref[0, h],
                                         axis=0, keepdims=True)
        s_wn = sn_ref[:, 0:1]
        m_w = jnp.maximum(jnp.max(sw, axis=1, keepdims=True), s_wn)
        p_w = jnp.where(valid_w, jnp.exp(sw - m_w), 0.0)
        p_wn = jnp.exp(s_wn - m_w)
        l_w = jnp.sum(p_w, axis=1, keepdims=True) + p_wn
        o_w = (_nt(p_w.astype(BF16), jnp.concatenate([vw.astype(BF16), zkw], axis=0))
               + p_wn * vnr_ref[0, kv, 1]) / l_w
        o_ref[0, kv] = (gate_ref[0, kv, 0] * o_cs[kv] + gate_ref[0, kv, 1] * o_s
                        + gate_ref[0, kv, 2] * o_w)
        for c in range(2):
            col = newb_ref[0, 512 + c * 128 + kv * 64:512 + c * 128 + (kv + 1) * 64, 0:1]
            rolled = pltpu.roll(swin_ref[0, 0, c, kv], wb - 1, 1)
            nwin_ref[0, c, kv] = jnp.where(lanew == wb - 1, col, rolled)


def _nsa_sample(layer, page_table, cnT, qb, qrow, newb, newrow, vnr, swT, gates, CS):
    DB, NP = page_table.shape
    P = NP * PAGE_SIZE
    wb = swT.shape[-1]
    NBP = -(-(P // CMP_BLOCK + 1) // 128) * 128
    assert NP % 4 == 0
    per_seq = lambda *shape: pl.BlockSpec((1,) + shape, lambda b, pt: (b,) + (0,) * len(shape))
    const = lambda a: pl.BlockSpec(a.shape, lambda b, pt: (0,) * a.ndim)
    consts = [CS["avg8"], CS["wbk"], CS["wbv"], CS["pem"], CS["expand"]]
    grid_spec = pltpu.PrefetchScalarGridSpec(
        num_scalar_prefetch=1, grid=(DB,),
        in_specs=[pl.BlockSpec(memory_space=pl.ANY),
                  per_seq(8, 64, 128), per_seq(2, 8, 128), per_seq(768, 128), per_seq(8, 256),
                  per_seq(2, 2, 8, 128),
                  pl.BlockSpec((1, 1, 2, 2, 64, wb), lambda b, pt: (layer, b, 0, 0, 0, 0)),
                  per_seq(2, 3, 8, 128)] + [const(a) for a in consts],
        out_specs=[per_seq(2, 8, 128), per_seq(2, 2, 64, wb)],
        scratch_shapes=[pltpu.VMEM((_NBUF, 3, 2, 64, 128), F32), pltpu.VMEM((8, P), F32),
                        pltpu.VMEM((NBP, 256), F32), pltpu.VMEM((8, 256), F32),
                        pltpu.VMEM((8, 64, 128), F32), pltpu.VMEM((8, 128), F32),
                        pltpu.SemaphoreType.DMA((_NBUF,))])
    return pl.pallas_call(
        functools.partial(_nsa_sample_kernel, layer=layer, n_pages=NP),
        grid_spec=grid_spec,
        out_shape=[jax.ShapeDtypeStruct((DB, 2, 8, 128), F32),
                   jax.ShapeDtypeStruct((DB, 2, 2, 64, wb), F32)],
        compiler_params=_cparams(1), name="nsa_sample",
    )(page_table, cnT, qb, qrow, newb, newrow, vnr, swT, gates, *consts)


_RING = 8


def _fox_decode_kernel(pt_ref, cf_ref, clf_ref, qb_ref, kn_ref, vn_ref, lfn_ref, o_ref,
                       ring, s_ref, lf_ref, w_ref, t_ref, acc_ref, sn_ref, sem, lsem, *, layer, n_pages):
    b = pl.program_id(0)
    NP = n_pages
    H = FOX_HEADS

    def page_copy(i, slot):
        c = i // NP
        pg = i - c * NP
        return pltpu.make_async_copy(cf_ref.at[layer, pt_ref[b, pg], c], ring.at[slot], sem.at[slot])

    def logf_copy(pg):
        return pltpu.make_async_copy(clf_ref.at[layer, pt_ref[b, pg]], lf_ref.at[pg], lsem)

    def start_logf(pg, c):
        logf_copy(pg).start()
        return c

    lax.fori_loop(0, NP, start_logf, 0)
    for s in range(_RING - 2):
        page_copy(s, s).start()

    def advance_pair(i):
        slots = [(i + u) % _RING for u in range(2)]
        for u in range(2):
            page_copy(i + u, slots[u]).wait()
        for u in range(2):
            nxt = i + u + _RING - 2

            @pl.when(nxt < 2 * NP)
            def _():
                page_copy(nxt, nxt % _RING).start()

        return slots

    def kbody(it, c):
        i = it * 2
        slots = advance_pair(i)
        for h in range(H):
            qh = qb_ref[0, h]
            for u in range(2):
                s_ref[i + u, h:h + 1, :] = jnp.sum(ring[slots[u], h] * qh, axis=0, keepdims=True)
        return c

    lax.fori_loop(0, NP // 2, kbody, 0)

    def wait_logf(pg, c):
        logf_copy(pg).wait()
        return c

    lax.fori_loop(0, NP, wait_logf, 0)

    ii = lax.broadcasted_iota(jnp.int32, (128, 128), 0)
    jj = lax.broadcasted_iota(jnp.int32, (128, 128), 1)
    later = jnp.where(ii > jj, 1.0, 0.0).astype(BF16)
    ones = jnp.ones((128, 128), BF16)
    l1, l2, l3 = _split3(lf_ref[...].reshape(NP * 8, 128))
    w_ref[...] = (_dot(l1, later) + _dot(l2, later) + _dot(l3, later)).reshape(NP, 8, 128)
    t_ref[...] = (_dot(l1, ones) + _dot(l2, ones) + _dot(l3, ones)).reshape(NP, 8, 128)
    for h in range(H):
        sn_ref[h:h + 1, :] = jnp.sum(kn_ref[0, h] * qb_ref[0, h], axis=0, keepdims=True)
    cn = lfn_ref[0]
    s_new = sn_ref[...] + cn - cn

    def sbody(k, carry):
        tot, mx = carry
        pg = NP - 1 - k
        sc = s_ref[pg] + cn + w_ref[pg] + tot
        s_ref[pg] = sc
        return tot + t_ref[pg], jnp.maximum(mx, sc)

    _, mx = lax.fori_loop(0, NP, sbody, (jnp.zeros((H, 128), F32), jnp.full((H, 128), NEG_INF, F32)))
    m = jnp.maximum(jnp.max(mx, axis=1, keepdims=True), s_new)
    p = jnp.exp(s_ref[...] - m[None])
    s_ref[...] = p
    p_new = jnp.exp(s_new - m)
    denom = jnp.sum(jnp.sum(p, axis=0), axis=1, keepdims=True) + p_new[:, 0:1]

    acc_ref[...] = jnp.zeros_like(acc_ref)

    def vbody(it, c):
        i = NP + it * 2
        slots = advance_pair(i)
        for h in range(H):
            a = acc_ref[h]
            for u in range(2):
                a = a + ring[slots[u], h] * s_ref[i - NP + u, h:h + 1, :]
            acc_ref[h] = a
        return c

    lax.fori_loop(0, NP // 2, vbody, 0)

    lane = lax.broadcasted_iota(jnp.int32, (HEAD_DIM, 128), 1)
    for h in range(H):
        a = acc_ref[h] + jnp.where(lane == 0, p_new[h:h + 1, :] * vn_ref[0, h], 0.0)
        o_ref[0, h:h + 1, :] = _lane_sum_rows(a)[0:1] / denom[h:h + 1]


def _fox_decode(layer, page_table, cfT, clfT, qb, knb, vnb, lfnb):
    DB, NP = page_table.shape
    assert NP % 2 == 0 and 2 * NP >= _RING
    per_seq = lambda *shape: pl.BlockSpec((1,) + shape, lambda b, pt: (b,) + (0,) * len(shape))
    page_f32 = pltpu.VMEM((NP, 8, 128), F32)
    grid_spec = pltpu.PrefetchScalarGridSpec(
        num_scalar_prefetch=1, grid=(DB,),
        in_specs=[pl.BlockSpec(memory_space=pl.ANY), pl.BlockSpec(memory_space=pl.ANY),
                  per_seq(8, 64, 128), per_seq(8, 64, 128), per_seq(8, 64, 128), per_seq(8, 128)],
        out_specs=per_seq(8, 64),
        scratch_shapes=[pltpu.VMEM((_RING, 8, 64, 128), F32), page_f32, page_f32, page_f32, page_f32,
                        pltpu.VMEM((8, 64, 128), F32), pltpu.VMEM((8, 128), F32),
                        pltpu.SemaphoreType.DMA((_RING,)), pltpu.SemaphoreType.DMA(())])
    return pl.pallas_call(
        functools.partial(_fox_decode_kernel, layer=layer, n_pages=NP),
        grid_spec=grid_spec, out_shape=jax.ShapeDtypeStruct((DB, 8, 64), F32),
        compiler_params=_cparams(1), name="fox_sample",
    )(page_table, cfT, clfT, qb, knb, vnb, lfnb)


def _nsa_decode_kernel(pt_ref, cn_ref, qb_ref, qrow_ref, newb_ref, newrow_ref, vnr_ref, swin_ref,
                       gate_ref, avg8_ref, wbk_ref, wbv_ref, pem_ref, exp_ref,
                       o_ref, nwin_ref,
                       ring, s_ref, m_ref, acc_ref, sn_ref, sem, *, layer, n_pages):
    b = pl.program_id(0)
    NP = n_pages
    G = NP // 4
    P = NP * PAGE_SIZE
    nbp = P // CMP_BLOCK
    NBP = m_ref.shape[0]
    H = NSA_HEADS
    wb = swin_ref.shape[-1]
    kch = exp_ref.shape[1]

    def copy1(pg, slot):
        return pltpu.make_async_copy(cn_ref.at[layer, pt_ref[b, pg], pl.ds(0, 3)], ring.at[slot], sem.at[slot])

    def copy2(pg, slot):
        return pltpu.make_async_copy(cn_ref.at[layer, pt_ref[b, pg], 3], ring.at[slot, 0], sem.at[slot])

    def start_group(gi):
        base = (gi % 2) * 4

        @pl.when(gi < G)
        def _():
            for u in range(4):
                copy1(gi * 4 + u, base + u).start()

        @pl.when((gi >= G) & (gi < 2 * G))
        def _():
            for u in range(4):
                copy2((gi - G) * 4 + u, base + u).start()

    start_group(jnp.int32(0))
    start_group(jnp.int32(1))
    m_ref[...] = jnp.zeros_like(m_ref)

    def body1(g, c):
        base = (g % 2) * 4
        contrib = jnp.zeros((8, 256), F32)
        for u in range(4):
            copy1(g * 4 + u, base + u).wait()
            x1, x2 = _split2(ring[base + u, 0:2].reshape(256, 128))
            contrib = contrib + _nt(avg8_ref[u], x1) + _nt(avg8_ref[u], x2)
        m_ref[pl.ds(pl.multiple_of(g * 8, 8), 8), :] = contrib
        for h in range(H):
            qh = qb_ref[0, h]
            for u in range(4):
                off = pl.multiple_of(g * 512 + u * 128, 128)
                s_ref[h:h + 1, pl.ds(off, 128)] = jnp.sum(ring[base + u, 2, h // NSA_GROUP] * qh,
                                                          axis=0, keepdims=True)
        start_group(g + 2)
        return c

    lax.fori_loop(0, G, body1, 0)

    rowm = lax.broadcasted_iota(jnp.int32, (NBP, 1), 0)
    pem = pem_ref[...]
    mm = m_ref[...] + jnp.where(rowm < nbp, pem, 0.0)
    mm = jnp.where(rowm == nbp, newrow_ref[0, 0:1, :] * (1.0 / CMP_BLOCK) + pem, mm)
    mb = mm.astype(BF16)
    n8 = lax.broadcasted_iota(jnp.int32, (8, NBP), 1)
    row8 = lax.broadcasted_iota(jnp.int32, (8, NBP), 0)
    rowo = lax.broadcasted_iota(jnp.int32, (8, 128), 0)
    cur = nbp
    kcb = [_dot(mb, wbk_ref[kv]).astype(BF16) for kv in range(NSA_KV_HEADS)]
    vcb = [_dot(mb, wbv_ref[kv]).astype(BF16) for kv in range(NSA_KV_HEADS)]
    sc = _nt(qrow_ref[0, 0], kcb[0]) + _nt(qrow_ref[0, 1], kcb[1])
    valid = (n8 + 1) * CMP_BLOCK - 1 <= P
    s = jnp.where(valid, sc, NEG_INF)
    e = jnp.exp(s - jnp.max(s, axis=1, keepdims=True))
    p_c = jnp.where(valid, e / jnp.sum(e, axis=1, keepdims=True), 0.0)
    o_c = (_dot(jnp.where(row8 < NSA_GROUP, p_c, 0.0).astype(BF16), vcb[0])
           + _dot(jnp.where(row8 >= NSA_GROUP, p_c, 0.0).astype(BF16), vcb[1]))
    imp0 = p_c[0:1] + p_c[1:2] + p_c[2:3] + p_c[3:4]
    imp1 = p_c[4:5] + p_c[5:6] + p_c[6:7] + p_c[7:8]
    n1 = n8[0:1]
    rowp = lax.broadcasted_iota(jnp.int32, (128, NBP), 0)
    m_idx = lax.broadcasted_iota(jnp.int32, (NBP, NBP), 0)
    n_idx = lax.broadcasted_iota(jnp.int32, (NBP, NBP), 1)
    imps = []
    for imp in (imp0, imp1):
        imp = jnp.where((n1 == cur) | (n1 == 0), FORCED, imp)
        imps.append(jnp.where(n1 > cur, -2.0, imp))
    impT = jnp.where(rowp == 0, imps[0], jnp.where(rowp == 1, imps[1], 0.0)).T
    sels = []
    for kv in range(NSA_KV_HEADS):
        col = impT[:, kv:kv + 1]
        beats = (col > imps[kv]) | ((col == imps[kv]) & (m_idx < n_idx))
        rank = jnp.sum(jnp.where(beats, 1.0, 0.0), axis=0, keepdims=True)
        sels.append(jnp.where((rank < N_SELECT) & (n1 <= cur), 1.0, 0.0))
    sel8 = jnp.where(row8 < NSA_GROUP, sels[0], sels[1]).astype(BF16)

    mx = jnp.full((H, 1), NEG_INF, F32)
    for ci in range(P // kch):
        mk = _dot(sel8[:, ci * 128:(ci + 1) * 128], exp_ref[...])
        sm = jnp.where(mk > 0.5, s_ref[:, ci * kch:(ci + 1) * kch], NEG_INF)
        s_ref[:, ci * kch:(ci + 1) * kch] = sm
        mx = jnp.maximum(mx, jnp.max(sm, axis=1, keepdims=True))
    for h in range(H):
        kv = h // NSA_GROUP
        sn_ref[h:h + 1, :] = jnp.sum(newb_ref[0, 256 + kv * 64:256 + (kv + 1) * 64, :] * qb_ref[0, h],
                                     axis=0, keepdims=True)
    s_new = sn_ref[...]
    m = jnp.maximum(mx, s_new)
    lsum = jnp.zeros((H, 1), F32)
    for ci in range(P // kch):
        p = jnp.exp(s_ref[:, ci * kch:(ci + 1) * kch] - m[:, 0:1])
        s_ref[:, ci * kch:(ci + 1) * kch] = p
        lsum = lsum + jnp.sum(p, axis=1, keepdims=True)
    p_new = jnp.exp(s_new - m)
    denom = lsum + p_new[:, 0:1]

    acc_ref[...] = jnp.zeros_like(acc_ref)

    def body2(g, c):
        base = (g % 2) * 4
        for u in range(4):
            copy2((g - G) * 4 + u, base + u).wait()
        for h in range(H):
            a = acc_ref[h]
            for u in range(4):
                off = pl.multiple_of((g - G) * 512 + u * 128, 128)
                a = a + ring[base + u, 0, h // NSA_GROUP] * s_ref[h:h + 1, pl.ds(off, 128)]
            acc_ref[h] = a
        start_group(g + 2)
        return c

    lax.fori_loop(G, 2 * G, body2, 0)

    zpad = jnp.zeros((HEAD_DIM, 128), F32)
    o_s = jnp.zeros((8, 128), F32)
    for h in range(H):
        r = _lane_sum_rows(jnp.concatenate([acc_ref[h], zpad], axis=0))
        o_s = jnp.where(rowo == h, r, o_s)
    o_s = (o_s + p_new[:, 0:1] * vnr_ref[0, 0]) / denom

    lane5 = lax.broadcasted_iota(jnp.int32, (8, wb), 1)
    row5 = lax.broadcasted_iota(jnp.int32, (8, wb), 0)
    zkw = jnp.zeros((HEAD_DIM, wb), BF16)
    sw = jnp.zeros((8, wb), F32)
    for kv in range(NSA_KV_HEADS):
        sw = sw + _dot(qrow_ref[0, kv], jnp.concatenate([swin_ref[0, 0, 0, kv].astype(BF16), zkw], axis=0))
    valid_w = (wb - lane5) < WINDOW
    sw = jnp.where(valid_w, sw, NEG_INF)
    for h in range(H):
        kv = h // NSA_GROUP
        sn_ref[h:h + 1, :] = jnp.sum(newb_ref[0, 512 + kv * 64:512 + (kv + 1) * 64, :] * qb_ref[0, h],
                                     axis=0, keepdims=True)
    s_wn = sn_ref[:, 0:1]
    m_w = jnp.maximum(jnp.max(sw, axis=1, keepdims=True), s_wn)
    p_w = jnp.where(valid_w, jnp.exp(sw - m_w), 0.0)
    p_wn = jnp.exp(s_wn - m_w)
    l_w = jnp.sum(p_w, axis=1, keepdims=True) + p_wn
    o_w = p_wn * vnr_ref[0, 1]
    for kv in range(NSA_KV_HEADS):
        mine = (row5 < NSA_GROUP) if kv == 0 else (row5 >= NSA_GROUP)
        o_w = o_w + _nt(jnp.where(mine, p_w, 0.0).astype(BF16),
                        jnp.concatenate([swin_ref[0, 0, 1, kv].astype(BF16), zkw], axis=0))
    o_w = o_w / l_w
    o_ref[0] = gate_ref[0, 0] * o_c + gate_ref[0, 1] * o_s + gate_ref[0, 2] * o_w

    lanew = lax.broadcasted_iota(jnp.int32, (HEAD_DIM, wb), 1)
    for kv in range(NSA_KV_HEADS):
        for c in range(2):
            col = newb_ref[0, 512 + c * 128 + kv * 64:512 + c * 128 + (kv + 1) * 64, 0:1]
            rolled = pltpu.roll(swin_ref[0, 0, c, kv], wb - 1, 1)
            nwin_ref[0, c, kv] = jnp.where(lanew == wb - 1, col, rolled)


def _nsa_decode(layer, page_table, cnT, qb, qrow, newb, newrow, vnr, swT, gates, CS):
    DB, NP = page_table.shape
    P = NP * PAGE_SIZE
    wb = swT.shape[-1]
    NBP = -(-(P // CMP_BLOCK + 1) // 128) * 128
    assert NP % 4 == 0 and NP >= 8
    per_seq = lambda *shape: pl.BlockSpec((1,) + shape, lambda b, pt: (b,) + (0,) * len(shape))
    const = lambda a: pl.BlockSpec(a.shape, lambda b, pt: (0,) * a.ndim)
    consts = [CS["avg8"], CS["wbk"], CS["wbv"], CS["pem"], CS["expand"]]
    grid_spec = pltpu.PrefetchScalarGridSpec(
        num_scalar_prefetch=1, grid=(DB,),
        in_specs=[pl.BlockSpec(memory_space=pl.ANY),
                  per_seq(8, 64, 128), per_seq(2, 8, 128), per_seq(768, 128), per_seq(8, 256),
                  per_seq(2, 8, 128),
                  pl.BlockSpec((1, 1, 2, 2, 64, wb), lambda b, pt: (layer, b, 0, 0, 0, 0)),
                  per_seq(3, 8, 128)] + [const(a) for a in consts],
        out_specs=[per_seq(8, 128), per_seq(2, 2, 64, wb)],
        scratch_shapes=[pltpu.VMEM((_RING, 3, 2, 64, 128), F32), pltpu.VMEM((8, P), F32),
                        pltpu.VMEM((NBP, 256), F32), pltpu.VMEM((8, 64, 128), F32),
                        pltpu.VMEM((8, 128), F32), pltpu.SemaphoreType.DMA((_RING,))])
    return pl.pallas_call(
        functools.partial(_nsa_decode_kernel, layer=layer, n_pages=NP),
        grid_spec=grid_spec,
        out_shape=[jax.ShapeDtypeStruct((DB, 8, 128), F32),
                   jax.ShapeDtypeStruct((DB, 2, 2, 64, wb), F32)],
        compiler_params=_cparams(1), name="nsa_sample",
    )(page_table, cnT, qb, qrow, newb, newrow, vnr, swT, gates, *consts)


def _sample_consts(P, L):
    r = jnp.arange(128)
    avg8 = jnp.stack([jnp.where(jnp.arange(8)[:, None] == 2 * p + r[None, :] // CMP_BLOCK,
                                1.0 / CMP_BLOCK, 0.0) for p in range(4)]).astype(BF16)
    w = L["w_cmp"]
    z = jnp.zeros((64, 128), F32)
    wpad = lambda c: jnp.concatenate([w[c], jnp.zeros((64, 64), F32)], axis=1)
    blocks = lambda c, kv: jnp.concatenate(
        [wpad(c) if (cc, kk) == (c, kv) else z for cc in range(2) for kk in range(2)], axis=0)
    wbk = jnp.stack([blocks(0, kv) for kv in range(2)]).astype(BF16)
    wbv = jnp.stack([blocks(1, kv) for kv in range(2)]).astype(BF16)
    pm = jnp.mean(L["pe"], axis=1)
    pem = jnp.concatenate([pm[0], pm[0], pm[1], pm[1]])[None, :]
    kch = min(P, 8192)
    expand = (r[:, None] == (jnp.arange(kch) // CMP_BLOCK)[None, :]).astype(BF16)
    return dict(avg8=avg8, wbk=wbk, wbv=wbv, pem=pem, expand=expand)


def _sample_layer(layer, xs, p_l, L, cos_s, sin_s, page_table, cnT, cfT, clfT, swT):
    DB, D = xs.shape
    P = page_table.shape[1] * PAGE_SIZE
    xpad = jnp.zeros((1, 128, D), F32).at[0, :DB].set(xs)
    nsaT, winT, foxT, logfT, qn, qf, kf, kvb, ga, gm, qT = _proj(
        xpad, L["gmix"], L["wt"], L["wg"], cos_s, sin_s, L["hg"], L["fb"], 128)
    lanes = lambda a: jnp.broadcast_to(a[..., None], a.shape + (128,))
    q_all = (qT[0, :, :DB] * SCALE).T
    qb_a = lanes(q_all[:, :512].reshape(DB, 8, 64))
    qb_f = lanes(q_all[:, 512:].reshape(DB, 8, 64))
    new = jnp.concatenate([nsaT[0], winT[0]], axis=0)[:, :DB].T
    fx = foxT[0][:, :DB].T
    lf = logfT[0][:, :DB].T
    q8 = jnp.transpose(qn[0, :, :DB, :], (1, 0, 2))
    zq = jnp.zeros_like(q8[:, :4])
    qrow = jnp.stack([jnp.concatenate([q8[:, :4], zq], axis=1),
                      jnp.concatenate([zq, q8[:, 4:]], axis=1)], axis=1)
    g = jnp.transpose(ga[0, :, :DB, :12].reshape(2, DB, 4, 3), (1, 3, 0, 2))
    gates = lanes(g.reshape(DB, 3, 8))
    vrow = jnp.stack([new[:, 384:512].reshape(DB, 2, 64), new[:, 640:768].reshape(DB, 2, 64)], axis=1)
    vrow = jnp.repeat(vrow, NSA_GROUP, axis=2)
    vnr = jnp.concatenate([vrow, jnp.zeros_like(vrow)], axis=-1)
    newrow = jnp.broadcast_to(new[:, None, :256], (DB, 8, 256))
    CS = _sample_consts(P, L)
    o_a8, new_win = _nsa_decode(layer, page_table, cnT, qb_a, qrow, lanes(new), newrow, vnr, swT, gates, CS)
    o_a = o_a8[:, :, :64].reshape(DB, 512)
    o_b = _fox_decode(layer, page_table, cfT, clfT, qb_f, lanes(fx[:, :512].reshape(DB, 8, 64)),
                      lanes(fx[:, 512:].reshape(DB, 8, 64)), lanes(lf)).reshape(DB, 512)
    y = _tail(xs, o_a, o_b, gm[0, :DB], p_l, L["tail"], DB)
    return (y, new[:, :512].reshape(DB, 1, 4, NSA_KV_HEADS, HEAD_DIM),
            fx.reshape(DB, 1, 2, FOX_HEADS, HEAD_DIM), lf.reshape(DB, 1, FOX_HEADS),
            jnp.transpose(new_win, (0, 4, 1, 2, 3)))


def kernel(x_prompt, x_sample, p_prompt, p_sample, cache_nsa, cache_fox, cache_fox_logf, state_win, page_table, norm_mix, w_in, q_norm_nsa, k_norm_nsa, q_norm_fox, k_norm_fox, fox_fbias, cmp_pe, w_cmp, w_up_nsa, w_up_fox, w_out, norm_ffn, w_ffn_gate, w_ffn_up, w_ffn_down, norm_ple, w_ple_gate, w_ple_proj):
    depth = w_in.shape[0]
    B, T, D = x_prompt.shape
    DB, S, _ = x_sample.shape
    assert S == 1 and DB <= 128
    P = page_table.shape[1] * PAGE_SIZE
    cnT = jnp.transpose(cache_nsa, (0, 1, 3, 4, 5, 2))
    cfT = jnp.transpose(cache_fox, (0, 1, 3, 4, 5, 2))
    clfT = jnp.transpose(cache_fox_logf, (0, 1, 3, 2))
    swT = jnp.transpose(state_win, (0, 1, 3, 4, 5, 2))
    C = _prompt_consts(T)
    cos_s, sin_s = _rope_tables(jnp.full((128,), P, jnp.int32))
    tm = min(256, T)
    xp, xs = x_prompt, x_sample.reshape(DB, D)
    outs = [[] for _ in range(8)]
    for l in range(depth):
        L = _prep_layer(l, norm_mix, w_in, q_norm_nsa, k_norm_nsa, q_norm_fox, k_norm_fox, fox_fbias,
                        cmp_pe, w_cmp, w_up_nsa, w_up_fox, w_out, norm_ffn, w_ffn_gate, w_ffn_up,
                        w_ffn_down, norm_ple, w_ple_gate, w_ple_proj)
        xp, nsa_p, fox_p, lf_p, win_p = _prompt_layer(xp, p_prompt[l], L, C, tm, tm, tm)
        xs, nsa_s, fox_s, lf_s, win_s = _sample_layer(l, xs, p_sample[l].reshape(DB, -1), L, cos_s, sin_s,
                                                      page_table, cnT, cfT, clfT, swT)
        for acc, v in zip(outs, (nsa_p, nsa_s, fox_p, fox_s, lf_p, lf_s, win_p, win_s)):
            acc.append(v)
    return (xp, xs.reshape(DB, 1, D)) + tuple(jnp.stack(o) for o in outs)
```

```python
import functools

import jax
import jax.numpy as jnp
import numpy as np
from jax import lax
from jax.experimental import pallas as pl
from jax.experimental.pallas import tpu as pltpu

HEAD_DIM = 64
NSA_HEADS = 8
NSA_KV_HEADS = 2
NSA_GROUP = NSA_HEADS // NSA_KV_HEADS
FOX_HEADS = 8
CMP_BLOCK = 64
N_SELECT = 16
WINDOW = 512
PAGE_SIZE = 128
ROPE_THETA = 10000.0
EPS = 1e-6
NEG_INF = -1e30
FORCED = 1e4
SCALE = HEAD_DIM ** -0.5

PROJ_TM = 256
FOX_TQ, FOX_TK = 512, 1024
NSA_CH = 1024
LANES = 128
VMEM_LIMIT = 56 * 1024 * 1024
UNSELECTED = NEG_INF

F32 = jnp.float32
BF16 = jnp.bfloat16

_R_QA, _R_KV6, _R_QB, _R_KB, _R_VB, _R_FB, _R_END = 0, 512, 1280, 1792, 2304, 2816, 2832


def _nt(a, b):
    return lax.dot_general(a, b, (((1,), (1,)), ((), ())), preferred_element_type=F32)


def _dot(a, b):
    return jnp.dot(a, b, preferred_element_type=F32)


def _split3(x):
    a = x.astype(BF16)
    r = x - a.astype(F32)
    b = r.astype(BF16)
    c = (r - b.astype(F32)).astype(BF16)
    return a, b, c


def _log_sigmoid(x):
    return jnp.minimum(x, 0.0) - jnp.log1p(jnp.exp(-jnp.abs(x)))


def _cparams(n_grid):
    return pltpu.CompilerParams(dimension_semantics=("arbitrary",) * n_grid,
                                vmem_limit_bytes=VMEM_LIMIT)


def _proj_kernel(x_ref, gmix_ref, wt_ref, wg_ref, cos_ref, sin_ref, hg_ref, fb_ref,
                 nsaT_ref, winT_ref, foxT_ref, logfT_ref, qn_ref, qf_ref, kf_ref, kvb_ref,
                 ga_ref, gm_ref, qT_ref, carry_ref, *, tm):
    j = pl.program_id(1)

    @pl.when(j == 0)
    def _():
        carry_ref[...] = jnp.zeros_like(carry_ref)

    x = x_ref[0]
    ms = jnp.mean(x * x, axis=-1, keepdims=True)
    h = (x * lax.rsqrt(ms + EPS) * gmix_ref[...]).astype(BF16)
    cos = cos_ref[...]
    sin = sin_ref[...]
    hg = hg_ref[...]

    def zt(r0, r1):
        return _nt(wt_ref[r0:r1, :], h)

    def headnorm(z, col):
        ss = jnp.sum(z * z, axis=0, keepdims=True)
        return z * lax.rsqrt(ss * (1.0 / HEAD_DIM) + EPS) * hg[:, col:col + 1]

    def rope(z):
        x1, x2 = z[:HEAD_DIM // 2], z[HEAD_DIM // 2:]
        return jnp.concatenate([x1 * cos - x2 * sin, x2 * cos + x1 * sin], axis=0)

    zeros64 = jnp.zeros((HEAD_DIM, tm), F32)
    row8 = lax.broadcasted_iota(jnp.int32, (8, tm), 0)

    for hp in range(NSA_HEADS // 2):
        z = zt(_R_QA + hp * 128, _R_QA + (hp + 1) * 128)
        for i in range(2):
            hd = hp * 2 + i
            q = rope(headnorm(z[i * 64:(i + 1) * 64], 0))
            qT_ref[0, hd * 64:(hd + 1) * 64, :] = q
            blk = jnp.concatenate([q * SCALE, zeros64], axis=0)
            qn_ref[0, hd] = blk.T.astype(BF16)

    for part in range(6):
        z = zt(_R_KV6 + part * 128, _R_KV6 + (part + 1) * 128)
        if part % 2 == 0:
            z = jnp.concatenate([rope(headnorm(z[:64], 1 + part // 2)),
                                 rope(headnorm(z[64:], 1 + part // 2))], axis=0)
        if part < 4:
            nsaT_ref[0, part * 128:(part + 1) * 128, :] = z
        else:
            winT_ref[0, (part - 4) * 128:(part - 3) * 128, :] = z
        if part >= 2:
            kvb_ref[0, (part - 2) * 128:(part - 1) * 128, :] = z.astype(BF16)

    zf = zt(_R_FB, _R_END)[:8]
    logf = _log_sigmoid(zf + fb_ref[...])
    logfT_ref[0] = logf
    ii = lax.broadcasted_iota(jnp.int32, (tm, tm), 0)
    jj = lax.broadcasted_iota(jnp.int32, (tm, tm), 1)
    tri = jnp.where(ii <= jj, 1.0, 0.0).astype(BF16)
    l1, l2, l3 = _split3(logf)
    c = carry_ref[:, 0:1] + (_dot(l1, tri) + _dot(l2, tri) + _dot(l3, tri))
    carry_ref[...] = jnp.broadcast_to(c[:, tm - 1:tm], carry_ref.shape)
    c1, c2, c3 = [p.astype(F32) for p in _split3(c)]

    for hp in range(FOX_HEADS // 2):
        z = zt(_R_QB + hp * 128, _R_QB + (hp + 1) * 128)
        for i in range(2):
            hd = hp * 2 + i
            q = headnorm(z[i * 64:(i + 1) * 64], 4)
            qT_ref[0, 512 + hd * 64:512 + (hd + 1) * 64, :] = q
            aug = jnp.where(row8 == 0, c1[hd:hd + 1],
                            jnp.where(row8 == 1, c2[hd:hd + 1],
                                      jnp.where(row8 == 2, c3[hd:hd + 1],
                                                jnp.where(row8 < 6, 1.0, 0.0))))
            blk = jnp.concatenate([q * SCALE, aug, jnp.zeros((56, tm), F32)], axis=0)
            qf_ref[0, hd] = blk.T.astype(BF16)

    for hp in range(FOX_HEADS // 2):
        z = zt(_R_KB + hp * 128, _R_KB + (hp + 1) * 128)
        for i in range(2):
            hd = hp * 2 + i
            k = headnorm(z[i * 64:(i + 1) * 64], 5)
            foxT_ref[0, hd * 64:(hd + 1) * 64, :] = k
            aug = jnp.where(row8 < 3, 1.0,
                            jnp.where(row8 == 3, -c1[hd:hd + 1],
                                      jnp.where(row8 == 4, -c2[hd:hd + 1],
                                                jnp.where(row8 == 5, -c3[hd:hd + 1], 0.0))))
            blk = jnp.concatenate([k, aug, jnp.zeros((56, tm), F32)], axis=0)
            kf_ref[0, hd] = blk.astype(BF16)

    for hp in range(FOX_HEADS // 2):
        z = zt(_R_VB + hp * 128, _R_VB + (hp + 1) * 128)
        foxT_ref[0, 512 + hp * 128:512 + (hp + 1) * 128, :] = z
        kvb_ref[0, 512 + hp * 128:512 + (hp + 1) * 128, :] = z.astype(BF16)

    zg = _dot(h, wg_ref[:, 0:256])
    ga_ref[0, 0] = jax.nn.sigmoid(zg[:, 0:128])
    ga_ref[0, 1] = jax.nn.sigmoid(zg[:, 128:256])
    ngm = gm_ref.shape[-1]
    for c0 in range(0, ngm, 512):
        gm_ref[0, :, c0:c0 + 512] = jax.nn.sigmoid(_dot(h, wg_ref[:, 256 + c0:256 + c0 + 512]))


def _proj(x, gmix, wt, wg, cos, sin, hg, fb, tm):
    B, T, D = x.shape
    grid = (B, T // tm)
    ngm = wg.shape[1] - 256
    const = lambda *shape: pl.BlockSpec(shape, lambda b, j: (0,) * len(shape))
    out_shape = [
        jax.ShapeDtypeStruct((B, 512, T), F32),
        jax.ShapeDtypeStruct((B, 256, T), F32),
        jax.ShapeDtypeStruct((B, 1024, T), F32),
        jax.ShapeDtypeStruct((B, 8, T), F32),
        jax.ShapeDtypeStruct((B, NSA_HEADS, T, 128), BF16),
        jax.ShapeDtypeStruct((B, FOX_HEADS, T, 128), BF16),
        jax.ShapeDtypeStruct((B, FOX_HEADS, 128, T), BF16),
        jax.ShapeDtypeStruct((B, 1024, T), BF16),
        jax.ShapeDtypeStruct((B, 2, T, 128), F32),
        jax.ShapeDtypeStruct((B, T, ngm), F32),
        jax.ShapeDtypeStruct((B, 1024, T), F32),
    ]
    out_specs = [
        pl.BlockSpec((1, 512, tm), lambda b, j: (b, 0, j)),
        pl.BlockSpec((1, 256, tm), lambda b, j: (b, 0, j)),
        pl.BlockSpec((1, 1024, tm), lambda b, j: (b, 0, j)),
        pl.BlockSpec((1, 8, tm), lambda b, j: (b, 0, j)),
        pl.BlockSpec((1, NSA_HEADS, tm, 128), lambda b, j: (b, 0, j, 0)),
        pl.BlockSpec((1, FOX_HEADS, tm, 128), lambda b, j: (b, 0, j, 0)),
        pl.BlockSpec((1, FOX_HEADS, 128, tm), lambda b, j: (b, 0, 0, j)),
        pl.BlockSpec((1, 1024, tm), lambda b, j: (b, 0, j)),
        pl.BlockSpec((1, 2, tm, 128), lambda b, j: (b, 0, j, 0)),
        pl.BlockSpec((1, tm, ngm), lambda b, j: (b, j, 0)),
        pl.BlockSpec((1, 1024, tm), lambda b, j: (b, 0, j)),
    ]
    in_specs = [
        pl.BlockSpec((1, tm, D), lambda b, j: (b, j, 0)),
        const(1, D),
        const(*wt.shape),
        const(*wg.shape),
        pl.BlockSpec((32, tm), lambda b, j: (0, j)),
        pl.BlockSpec((32, tm), lambda b, j: (0, j)),
        const(64, 8),
        const(8, 1),
    ]
    return pl.pallas_call(
        functools.partial(_proj_kernel, tm=tm),
        grid=grid, in_specs=in_specs, out_specs=out_specs, out_shape=out_shape,
        scratch_shapes=[pltpu.VMEM((8, 128), F32)],
        compiler_params=_cparams(2), name="proj",
    )(x, gmix, wt, wg, cos, sin, hg, fb)


def _fox_kernel(q_ref, k_ref, v_ref, o_ref, *, tq, tk):
    j = pl.program_id(2)
    row = lax.broadcasted_iota(jnp.int32, (128, tk), 0)
    lane_o = lax.broadcasted_iota(jnp.int32, (tq, 128), 1)
    one_row = (64, 0)
    keep = [jnp.where((row < 64) if i == 0 else (row >= 64), 1.0, 0.0).astype(BF16) for i in range(2)]
    fill = [jnp.where(row == one_row[i], 1.0, 0.0).astype(BF16) for i in range(2)]
    qs = [q_ref[0, i] for i in range(2)]

    def step(c, carry, causal):
        off = pl.multiple_of(c * tk, tk)
        vs = v_ref[0, :, pl.ds(off, tk)]
        out = []
        for i in range(2):
            m, acc = carry[i]
            s = _dot(qs[i], k_ref[0, i, :, pl.ds(off, tk)])
            if causal:
                rr = j * tq + lax.broadcasted_iota(jnp.int32, (tq, tk), 0)
                cc = c * tk + lax.broadcasted_iota(jnp.int32, (tq, tk), 1)
                s = jnp.where(cc <= rr, s, NEG_INF)
            m_new = jnp.maximum(m, jnp.max(s, axis=1, keepdims=True))
            p = jnp.exp(s - m_new).astype(BF16)
            acc = jnp.exp(m - m_new) * acc + _nt(p, vs * keep[i] + fill[i])
            out.append((m_new, acc))
        return tuple(out)

    init = tuple((jnp.full((tq, 1), NEG_INF, F32), jnp.zeros((tq, 128), F32)) for _ in range(2))
    last = (j * tq) // tk
    carry = lax.fori_loop(0, last, functools.partial(step, causal=False), init)
    (_, a0), (_, a1) = step(last, carry, True)
    o_ref[0] = jnp.where(lane_o < 64, a0 / a0[:, 64:65], a1 / a1[:, 0:1])


def _fox_prompt(qf, kf, kvb, tq):
    B, H, T, _ = qf.shape
    grid = (B, H // 2, T // tq)
    tk = min(FOX_TK, T)
    assert T % tk == 0 and tk % tq == 0
    return pl.pallas_call(
        functools.partial(_fox_kernel, tq=tq, tk=tk),
        grid=grid,
        in_specs=[
            pl.BlockSpec((1, 2, tq, 128), lambda b, hp, j: (b, hp, j, 0)),
            pl.BlockSpec((1, 2, 128, T), lambda b, hp, j: (b, hp, 0, 0)),
            pl.BlockSpec((1, 128, T), lambda b, hp, j: (b, 4 + hp, 0)),
        ],
        out_specs=pl.BlockSpec((1, tq, 128), lambda b, hp, j: (b, j, hp)),
        out_shape=jax.ShapeDtypeStruct((B, T, 512), F32),
        compiler_params=_cparams(3), name="fox_prompt",
    )(qf, kf, kvb)


def _cmp_kernel(xk_ref, xv_ref, avg_ref, avgT_ref, peT_ref, pe_ref, w0T_ref, w1_ref,
                kcb_ref, vcb_ref):
    lane = lax.broadcasted_iota(jnp.int32, (128, 128), 1)
    rowi = lax.broadcasted_iota(jnp.int32, (128, 128), 0)
    xk = xk_ref[0]
    k1, k2, _ = _split3(xk)
    mk = _dot(k1, avg_ref[...]) + _dot(k2, avg_ref[...])
    mk = mk + jnp.mean(peT_ref[0], axis=1, keepdims=True)
    kcb = _dot(w0T_ref[...], mk.astype(BF16))
    kcb_ref[0, 0] = jnp.where(lane >= 64, kcb, 0.0).astype(BF16)
    xv = xv_ref[0]
    v1, v2, _ = _split3(xv)
    mv = _nt(avgT_ref[...], v1) + _nt(avgT_ref[...], v2)
    mv = mv + jnp.mean(pe_ref[1], axis=0, keepdims=True)
    vcb = _dot(mv.astype(BF16), w1_ref[...])
    vcb_ref[0, 0] = jnp.where(rowi >= 64, vcb, 0.0).astype(BF16)


def _cmp_prompt(nsaT, avg, avgT, peT, pe, w0T, w1):
    B, _, T = nsaT.shape
    const = lambda *shape: pl.BlockSpec(shape, lambda b, kv: (0,) * len(shape))
    return pl.pallas_call(
        _cmp_kernel,
        grid=(B, NSA_KV_HEADS),
        in_specs=[
            pl.BlockSpec((1, 64, T), lambda b, kv: (b, kv, 0)),
            pl.BlockSpec((1, 64, T), lambda b, kv: (b, 2 + kv, 0)),
            const(T, 128), const(128, T), const(2, 64, 64), const(2, 64, 64),
            const(128, 64), const(64, 128),
        ],
        out_specs=[pl.BlockSpec((1, 1, 128, 128), lambda b, kv: (b, kv, 0, 0))] * 2,
        out_shape=[jax.ShapeDtypeStruct((B, NSA_KV_HEADS, 128, 128), BF16)] * 2,
        compiler_params=_cparams(2), name="cmp_prompt",
    )(nsaT, nsaT, avg, avgT, peT, pe, w0T, w1)


def _nsa_kernel(q_ref, kcb_ref, vcb_ref, ks_ref, vs_ref, kw_ref, vw_ref, oh_ref, ga_ref,
                o_ref, *, nb, ch, ww, n_tiles):
    j = pl.program_id(2)
    QB = 128
    R = NSA_GROUP * QB
    q4 = q_ref[0].reshape(R, 128)
    lane = lax.broadcasted_iota(jnp.int32, (QB, 128), 1)
    n_idx = lane - 64
    r_q = lax.broadcasted_iota(jnp.int32, (QB, 128), 0)
    t = j * QB + r_q
    real = (lane >= 64) & (n_idx < nb)
    rep4 = lambda a: jnp.concatenate([a] * NSA_GROUP, axis=0)
    lane4 = lax.broadcasted_iota(jnp.int32, (R, 128), 1)
    t4 = j * QB + lax.broadcasted_iota(jnp.int32, (R, 128), 0) % QB

    s_c = _dot(q4, kcb_ref[0, 0])
    valid_c = (lane4 >= 64) & (lane4 - 64 < nb) & ((lane4 - 63) * CMP_BLOCK - 1 <= t4)
    s = jnp.where(valid_c, s_c, NEG_INF)
    e = jnp.exp(s - jnp.max(s, axis=1, keepdims=True))
    p_c = jnp.where(valid_c, e / jnp.sum(e, axis=1, keepdims=True), 0.0)
    o_c = _dot(p_c.astype(BF16), vcb_ref[0, 0])

    imp = p_c[0:QB] + p_c[QB:2 * QB] + p_c[2 * QB:3 * QB] + p_c[3 * QB:4 * QB]
    cur = t // CMP_BLOCK
    imp = jnp.where((n_idx == cur) | (n_idx == 0), FORCED, jnp.where(n_idx > cur, -1.0, imp))
    imp = jnp.where(real, imp, -2.0)
    impT = imp.T[64:128]
    n_row = lax.broadcasted_iota(jnp.int32, (64, QB), 0)
    curT = (j * QB + lax.broadcasted_iota(jnp.int32, (64, QB), 1)) // CMP_BLOCK
    rankT = jnp.zeros((64, QB), jnp.int32)
    for mm in range(nb):
        rowm = impT[mm:mm + 1, :]
        beats = (rowm > impT) | ((rowm == impT) & (n_row > mm))
        rankT = rankT + beats.astype(jnp.int32)
    selT = (rankT < N_SELECT) & (n_row <= curT) & (n_row < nb)
    biasT = jnp.where(selT, 0.0, UNSELECTED)
    bias = jnp.concatenate([jnp.zeros((64, QB), F32), biasT], axis=0).T
    q_sel = (q4.astype(F32) + rep4(bias)).astype(BF16)

    def ones_rows(width):
        return jnp.where(lax.broadcasted_iota(jnp.int32, (64, width), 0) == 0, 1.0, 0.0).astype(BF16)

    def attend(q, k_ref_, v_ref_, off, width, bias_q, carry):
        m, acc = carry
        k_aug = jnp.concatenate([k_ref_[0, :, pl.ds(off, width)], oh_ref[:, pl.ds(off, width)]], axis=0)
        s = _dot(q, k_aug)
        if bias_q is not None:
            s = s + rep4(bias_q)
        m_new = jnp.maximum(m, jnp.max(s, axis=1, keepdims=True))
        p = jnp.exp(s - m_new).astype(BF16)
        v_aug = jnp.concatenate([v_ref_[0, :, pl.ds(off, width)], ones_rows(width)], axis=0)
        acc = jnp.exp(m - m_new) * acc + _nt(p, v_aug)
        return m_new, acc

    init = (jnp.full((R, 1), NEG_INF, F32), jnp.zeros((R, 128), F32))

    last = (j * QB) // ch
    carry = lax.fori_loop(
        0, last, lambda c, cr: attend(q_sel, ks_ref, vs_ref, pl.multiple_of(c * ch, ch), ch, None, cr), init)
    tq_c = j * QB + lax.broadcasted_iota(jnp.int32, (QB, ch), 0)
    kp_c = last * ch + lax.broadcasted_iota(jnp.int32, (QB, ch), 1)
    _, acc_s = attend(q_sel, ks_ref, vs_ref, pl.multiple_of(last * ch, ch), ch,
                      jnp.where(kp_c <= tq_c, 0.0, NEG_INF), carry)
    o_s = acc_s / acc_s[:, 64:65]

    ws = jnp.clip(j - WINDOW // QB, 0, n_tiles - ww // QB)
    dpos = (j - ws) * QB + lax.broadcasted_iota(jnp.int32, (QB, ww), 0) \
        - lax.broadcasted_iota(jnp.int32, (QB, ww), 1)
    _, acc_w = attend(q4, kw_ref, vw_ref, pl.multiple_of(ws * QB, QB), ww,
                      jnp.where((dpos >= 0) & (dpos < WINDOW), 0.0, NEG_INF), init)
    o_w = acc_w / acc_w[:, 64:65]

    ga = ga_ref[0, 0]
    outs = []
    for g in range(NSA_GROUP):
        sl = slice(g * QB, (g + 1) * QB)
        outs.append(ga[:, 3 * g:3 * g + 1] * o_c[sl] + ga[:, 3 * g + 1:3 * g + 2] * o_s[sl]
                    + ga[:, 3 * g + 2:3 * g + 3] * o_w[sl])
    lo = lane < 64
    o_ref[0, :, 0:128] = jnp.where(lo, outs[0], pltpu.roll(outs[1], 64, 1))
    o_ref[0, :, 128:256] = jnp.where(lo, outs[2], pltpu.roll(outs[3], 64, 1))


def _nsa_prompt(qn, kcb, vcb, kvb, onehot, ga):
    B, H, T, _ = qn.shape
    nb = T // CMP_BLOCK
    assert nb <= 64
    QB = 128
    grid = (B, NSA_KV_HEADS, T // QB)
    kvspec = lambda base: pl.BlockSpec((1, 64, T), lambda b, kv, j: (b, base + kv, 0))
    return pl.pallas_call(
        functools.partial(_nsa_kernel, nb=nb, ch=min(NSA_CH, T), ww=min(WINDOW + QB, T), n_tiles=T // QB),
        grid=grid,
        in_specs=[
            pl.BlockSpec((1, NSA_GROUP, QB, 128), lambda b, kv, j: (b, kv, j, 0)),
            pl.BlockSpec((1, 1, 128, 128), lambda b, kv, j: (b, kv, 0, 0)),
            pl.BlockSpec((1, 1, 128, 128), lambda b, kv, j: (b, kv, 0, 0)),
            kvspec(0), kvspec(2), kvspec(4), kvspec(6),
            pl.BlockSpec((64, T), lambda b, kv, j: (0, 0)),
            pl.BlockSpec((1, 1, QB, 128), lambda b, kv, j: (b, kv, j, 0)),
        ],
        out_specs=pl.BlockSpec((1, QB, 256), lambda b, kv, j: (b, j, kv)),
        out_shape=jax.ShapeDtypeStruct((B, T, 512), F32),
        compiler_params=_cparams(3), name="nsa_prompt",
    )(qn, kcb, vcb, kvb, kvb, kvb, kvb, onehot, ga)


def _tail_kernel(x_ref, oa_ref, ob_ref, gm_ref, p_ref, wua_ref, wub_ref, wo_ref, nf_ref,
                 wg_ref, wu_ref, wd_ref, npl_ref, wpg_ref, wpp_ref, y_ref):
    D = x_ref.shape[-1]
    u_a = _dot(oa_ref[...].astype(BF16), wua_ref[...])
    u_b = _dot(ob_ref[...].astype(BF16), wub_ref[...])
    mixed = gm_ref[:, 0:D] * u_a + gm_ref[:, D:2 * D] * u_b
    x1 = x_ref[...] + _dot(mixed.astype(BF16), wo_ref[...])

    def rms(v, g):
        return (v * lax.rsqrt(jnp.mean(v * v, axis=-1, keepdims=True) + EPS) * g).astype(BF16)

    h = rms(x1, nf_ref[...])
    ff = jax.nn.silu(_dot(h, wg_ref[...])) * _dot(h, wu_ref[...])
    x2 = x1 + _dot(ff.astype(BF16), wd_ref[...])
    gate = jax.nn.sigmoid(_dot(rms(x2, npl_ref[...]), wpg_ref[...]))
    y_ref[...] = x2 + gate * _dot(p_ref[...].astype(BF16), wpp_ref[...])


def _tail(x, oa, ob, gm, p, w, tm):
    N, D = x.shape
    tok = lambda width: pl.BlockSpec((tm, width), lambda i: (i, 0))
    const = lambda a: pl.BlockSpec(a.shape, lambda i: (0,) * a.ndim, pipeline_mode=pl.Buffered(1))
    return pl.pallas_call(
        _tail_kernel,
        grid=(N // tm,),
        in_specs=[tok(D), tok(oa.shape[1]), tok(ob.shape[1]), tok(gm.shape[1]), tok(p.shape[1])]
        + [const(a) for a in w],
        out_specs=tok(D),
        out_shape=jax.ShapeDtypeStruct((N, D), F32),
        compiler_params=_cparams(1), name="tail",
    )(x, oa, ob, gm, p, *w)


def _prep_layer(l, norm_mix, w_in, q_norm_nsa, k_norm_nsa, q_norm_fox, k_norm_fox, fox_fbias,
                cmp_pe, w_cmp, w_up_nsa, w_up_fox, w_out, norm_ffn, w_ffn_gate, w_ffn_up,
                w_ffn_down, norm_ple, w_ple_gate, w_ple_proj):
    wl = w_in[l]
    D = wl.shape[0]
    o_ga = 512 + 6 * 128
    o_qb = o_ga + 3 * NSA_HEADS
    o_fb = o_qb + 3 * 512
    o_gm = o_fb + FOX_HEADS
    wt = jnp.concatenate([wl[:, 0:o_ga], wl[:, o_qb:o_gm], jnp.zeros((D, 8), F32)], axis=1).T.astype(BF16)
    zpad = jnp.zeros((D, 128 - 12), F32)
    wg = jnp.concatenate([wl[:, o_ga:o_ga + 12], zpad, wl[:, o_ga + 12:o_ga + 24], zpad, wl[:, o_gm:]],
                         axis=1).astype(BF16)
    hg = jnp.stack([q_norm_nsa[l], k_norm_nsa[l, 0], k_norm_nsa[l, 1], k_norm_nsa[l, 2],
                    q_norm_fox[l], k_norm_fox[l], jnp.zeros((64,), F32), jnp.zeros((64,), F32)], axis=1)
    z64 = jnp.zeros((64, 64), F32)
    return dict(
        gmix=norm_mix[l][None, :], wt=wt, wg=wg, hg=hg, fb=fox_fbias[l][:, None],
        pe=cmp_pe[l], peT=jnp.swapaxes(cmp_pe[l], 1, 2),
        w0T=jnp.concatenate([w_cmp[l, 0].T, z64], axis=0).astype(BF16),
        w1=jnp.concatenate([w_cmp[l, 1], z64], axis=1).astype(BF16),
        w_cmp=w_cmp[l],
        tail=(w_up_nsa[l].astype(BF16), w_up_fox[l].astype(BF16), w_out[l].astype(BF16),
              norm_ffn[l][None, :], w_ffn_gate[l].astype(BF16), w_ffn_up[l].astype(BF16),
              w_ffn_down[l].astype(BF16), norm_ple[l][None, :], w_ple_gate[l].astype(BF16),
              w_ple_proj[l].astype(BF16)),
    )


def _rope_tables(pos):
    half = HEAD_DIM // 2
    inv_freq = ROPE_THETA ** (-jnp.arange(half, dtype=F32) / half)
    ang = inv_freq[:, None] * pos.astype(F32)[None, :]
    return jnp.cos(ang), jnp.sin(ang)


def _prompt_consts(T):
    tpos = jnp.arange(T, dtype=jnp.int32)
    blk = tpos // CMP_BLOCK
    avg = jnp.where(jnp.arange(128)[None, :] == 64 + blk[:, None], 1.0 / CMP_BLOCK, 0.0).astype(BF16)
    onehot = (jnp.arange(64)[:, None] == blk[None, :]).astype(BF16)
    cos, sin = _rope_tables(tpos)
    return dict(avg=avg, avgT=avg.T, onehot=onehot, cos=cos, sin=sin)


def _prompt_layer(xp, p_l, L, C, tm_proj, tq_fox, tm_tail):
    B, T, D = xp.shape
    nsaT, winT, foxT, logfT, qn, qf, kf, kvb, ga, gm, _ = _proj(
        xp, L["gmix"], L["wt"], L["wg"], C["cos"], C["sin"], L["hg"], L["fb"], tm_proj)
    kcb, vcb = _cmp_prompt(nsaT, C["avg"], C["avgT"], L["peT"], L["pe"], L["w0T"], L["w1"])
    o_a = _nsa_prompt(qn, kcb, vcb, kvb, C["onehot"], ga)
    o_b = _fox_prompt(qf, kf, kvb, tq_fox)
    N = B * T
    y = _tail(xp.reshape(N, D), o_a.reshape(N, 512), o_b.reshape(N, 512), gm.reshape(N, gm.shape[-1]),
              p_l.reshape(N, p_l.shape[-1]), L["tail"], tm_tail).reshape(B, T, D)
    nsa_rows = jnp.transpose(nsaT.reshape(B, 4, NSA_KV_HEADS, HEAD_DIM, T), (0, 4, 1, 2, 3))
    fox_rows = jnp.transpose(foxT.reshape(B, 2, FOX_HEADS, HEAD_DIM, T), (0, 4, 1, 2, 3))
    logf = jnp.transpose(logfT, (0, 2, 1))
    wk = min(WINDOW, T)
    win_rows = jnp.transpose(winT[:, :, T - wk:].reshape(B, 2, NSA_KV_HEADS, HEAD_DIM, wk), (0, 4, 1, 2, 3))
    return y, nsa_rows, fox_rows, logf, win_rows


_NBUF = 6


def _lane_sum_rows(a):
    ones = jnp.ones((8, a.shape[1]), BF16)
    a1, a2, a3 = _split3(a)
    return _nt(ones, a1) + _nt(ones, a2) + _nt(ones, a3)


def _fox_sample_kernel(pt_ref, cf_ref, clf_ref, qb_ref, kn_ref, vn_ref, lfn_ref, o_ref,
                       ring, s_ref, lf_ref, acc_ref, sn_ref, sem, lsem, *, layer, n_pages):
    b = pl.program_id(0)
    NP = n_pages
    H = FOX_HEADS

    def page_copy(i, slot):
        c = i // NP
        pg = i - c * NP
        return pltpu.make_async_copy(cf_ref.at[layer, pt_ref[b, pg], c], ring.at[slot], sem.at[slot])

    def logf_copy(pg):
        return pltpu.make_async_copy(clf_ref.at[layer, pt_ref[b, pg]],
                                     lf_ref.at[:, pl.ds(pl.multiple_of(pg * 128, 128), 128)], lsem)

    def start_logf(pg, c):
        logf_copy(pg).start()
        return c

    lax.fori_loop(0, NP, start_logf, 0)
    for s in range(_NBUF - 1):
        page_copy(s, s).start()

    def advance(i):
        slot = i % _NBUF
        page_copy(i, slot).wait()
        nxt = i + _NBUF - 1

        @pl.when(nxt < 2 * NP)
        def _():
            page_copy(nxt, nxt % _NBUF).start()

        return slot

    def kbody(i, c):
        slot = advance(i)
        off = pl.multiple_of(i * 128, 128)
        for h in range(H):
            s_ref[h:h + 1, pl.ds(off, 128)] = jnp.sum(ring[slot, h] * qb_ref[0, h], axis=0, keepdims=True)
        return c

    lax.fori_loop(0, NP, kbody, 0)

    def wait_logf(pg, c):
        logf_copy(pg).wait()
        return c

    lax.fori_loop(0, NP, wait_logf, 0)

    for h in range(H):
        sn_ref[h:h + 1, :] = jnp.sum(kn_ref[0, h] * qb_ref[0, h], axis=0, keepdims=True)
    s_new = sn_ref[...]
    cn = lfn_ref[0]
    ii = lax.broadcasted_iota(jnp.int32, (128, 128), 0)
    jj = lax.broadcasted_iota(jnp.int32, (128, 128), 1)
    later = jnp.where(ii > jj, 1.0, 0.0).astype(BF16)

    def sbody(k, carry):
        tot, mx = carry
        off = pl.multiple_of((NP - 1 - k) * 128, 128)
        chunk = lf_ref[:, pl.ds(off, 128)]
        l1, l2, l3 = _split3(chunk)
        within = _dot(l1, later) + _dot(l2, later) + _dot(l3, later)
        sc = s_ref[:, pl.ds(off, 128)] + cn + within + tot
        s_ref[:, pl.ds(off, 128)] = sc
        return tot + jnp.sum(chunk, axis=1, keepdims=True), jnp.maximum(mx, jnp.max(sc, axis=1, keepdims=True))

    _, mx = lax.fori_loop(0, NP, sbody, (jnp.zeros((H, 1), F32), jnp.full((H, 1), NEG_INF, F32)))
    m = jnp.maximum(mx, s_new + cn - cn)

    def pbody(k, lsum):
        off = pl.multiple_of(k * 128, 128)
        p = jnp.exp(s_ref[:, pl.ds(off, 128)] - m)
        s_ref[:, pl.ds(off, 128)] = p
        return lsum + p

    lsum = lax.fori_loop(0, NP, pbody, jnp.zeros((H, 128), F32))
    p_new = jnp.exp(s_new + cn - cn - m)
    denom = jnp.sum(lsum, axis=1, keepdims=True) + p_new[:, 0:1]

    acc_ref[...] = jnp.zeros_like(acc_ref)

    def vbody(i, c):
        slot = advance(i)
        off = pl.multiple_of((i - NP) * 128, 128)
        for h in range(H):
            acc_ref[h] = acc_ref[h] + ring[slot, h] * s_ref[h:h + 1, pl.ds(off, 128)]
        return c

    lax.fori_loop(NP, 2 * NP, vbody, 0)

    lane = lax.broadcasted_iota(jnp.int32, (HEAD_DIM, 128), 1)
    for h in range(H):
        a = acc_ref[h] + jnp.where(lane == 0, p_new[h:h + 1, :] * vn_ref[0, h], 0.0)
        o_ref[0, h:h + 1, :] = _lane_sum_rows(a)[0:1] / denom[h:h + 1]


def _fox_sample(layer, page_table, cfT, clfT, qb, knb, vnb, lfnb):
    DB, NP = page_table.shape
    P = NP * PAGE_SIZE
    per_seq = lambda *shape: pl.BlockSpec((1,) + shape, lambda b, pt: (b,) + (0,) * len(shape))
    grid_spec = pltpu.PrefetchScalarGridSpec(
        num_scalar_prefetch=1, grid=(DB,),
        in_specs=[pl.BlockSpec(memory_space=pl.ANY), pl.BlockSpec(memory_space=pl.ANY),
                  per_seq(8, 64, 128), per_seq(8, 64, 128), per_seq(8, 64, 128), per_seq(8, 128)],
        out_specs=per_seq(8, 64),
        scratch_shapes=[pltpu.VMEM((_NBUF, 8, 64, 128), F32), pltpu.VMEM((8, P), F32),
                        pltpu.VMEM((8, P), F32), pltpu.VMEM((8, 64, 128), F32), pltpu.VMEM((8, 128), F32),
                        pltpu.SemaphoreType.DMA((_NBUF,)), pltpu.SemaphoreType.DMA(())])
    return pl.pallas_call(
        functools.partial(_fox_sample_kernel, layer=layer, n_pages=NP),
        grid_spec=grid_spec, out_shape=jax.ShapeDtypeStruct((DB, 8, 64), F32),
        compiler_params=_cparams(1), name="fox_sample",
    )(page_table, cfT, clfT, qb, knb, vnb, lfnb)


def _split2(x):
    a = x.astype(BF16)
    return a, (x - a.astype(F32)).astype(BF16)


def _nsa_sample_kernel(pt_ref, cn_ref, qb_ref, qrow_ref, newb_ref, newrow_ref, vnr_ref, swin_ref,
                       gate_ref, avg8_ref, wbk_ref, wbv_ref, pem_ref, exp_ref,
                       o_ref, nwin_ref,
                       ring, s_ref, m_ref, m8_ref, acc_ref, sn_ref, sem, *, layer, n_pages):
    b = pl.program_id(0)
    NP = n_pages
    P = NP * PAGE_SIZE
    nbp = P // CMP_BLOCK
    NBP = m_ref.shape[0]
    H = NSA_HEADS
    wb = swin_ref.shape[-1]
    kch = exp_ref.shape[1]

    def copy1(i, slot):
        return pltpu.make_async_copy(cn_ref.at[layer, pt_ref[b, i], pl.ds(0, 3)], ring.at[slot], sem.at[slot])

    def copy2(i, slot):
        return pltpu.make_async_copy(cn_ref.at[layer, pt_ref[b, i - NP], 3], ring.at[slot, 0], sem.at[slot])

    def prefetch(nxt):
        @pl.when(nxt < NP)
        def _():
            copy1(nxt, nxt % _NBUF).start()

        @pl.when((nxt >= NP) & (nxt < 2 * NP))
        def _():
            copy2(nxt, nxt % _NBUF).start()

    for s in range(_NBUF - 1):
        prefetch(jnp.int32(s))
    m_ref[...] = jnp.zeros_like(m_ref)

    def body1(i, c):
        slot = i % _NBUF
        copy1(i, slot).wait()
        prefetch(i + _NBUF - 1)
        x1, x2 = _split2(ring[slot, 0:2].reshape(256, 128))
        a = avg8_ref[i % 4]
        contrib = _nt(a, x1) + _nt(a, x2)

        @pl.when(i % 4 == 0)
        def _():
            m8_ref[...] = contrib

        @pl.when(i % 4 != 0)
        def _():
            m8_ref[...] = m8_ref[...] + contrib

        @pl.when(i % 4 == 3)
        def _():
            m_ref[pl.ds(pl.multiple_of((i // 4) * 8, 8), 8), :] = m8_ref[...]

        off = pl.multiple_of(i * 128, 128)
        for h in range(H):
            s_ref[h:h + 1, pl.ds(off, 128)] = jnp.sum(ring[slot, 2, h // NSA_GROUP] * qb_ref[0, h],
                                                      axis=0, keepdims=True)
        return c

    lax.fori_loop(0, NP, body1, 0)

    rowm = lax.broadcasted_iota(jnp.int32, (NBP, 1), 0)
    pem = pem_ref[...]
    mm = m_ref[...] + jnp.where(rowm < nbp, pem, 0.0)
    mm = jnp.where(rowm == nbp, newrow_ref[0, 0:1, :] * (1.0 / CMP_BLOCK) + pem, mm)
    mb = mm.astype(BF16)
    n8 = lax.broadcasted_iota(jnp.int32, (8, NBP), 1)
    n1 = n8[0:1]
    row8 = lax.broadcasted_iota(jnp.int32, (8, NBP), 0)
    rowo = lax.broadcasted_iota(jnp.int32, (8, 128), 0)
    cur = nbp
    o_cs, sels = [], []
    for kv in range(NSA_KV_HEADS):
        kcb = _dot(mb, wbk_ref[kv]).astype(BF16)
        vcb = _dot(mb, wbv_ref[kv]).astype(BF16)
        sc = _nt(qrow_ref[0, kv], kcb)
        valid = (n8 + 1) * CMP_BLOCK - 1 <= P
        s = jnp.where(valid, sc, NEG_INF)
        e = jnp.exp(s - jnp.max(s, axis=1, keepdims=True))
        p_c = jnp.where(valid, e / jnp.sum(e, axis=1, keepdims=True), 0.0)
        o_cs.append(_dot(p_c.astype(BF16), vcb))
        imp = p_c[0:1] + p_c[1:2] + p_c[2:3] + p_c[3:4]
        imp = jnp.where((n1 == cur) | (n1 == 0), FORCED, imp)
        v = jnp.where(n1 > cur, -2.0, imp)
        sel = jnp.zeros((1, NBP), F32)
        for _ in range(N_SELECT):
            mx = jnp.max(v, axis=1, keepdims=True)
            idx = jnp.min(jnp.where(v == mx, n1, 1 << 30), axis=1, keepdims=True)
            hit = n1 == idx
            sel = jnp.where(hit, 1.0, sel)
            v = jnp.where(hit, -3.0, v)
        sels.append(jnp.where(n1 <= cur, sel, 0.0))
    sel8 = jnp.where(row8 < NSA_GROUP, sels[0], sels[1]).astype(BF16)

    mx = jnp.full((H, 1), NEG_INF, F32)
    for ci in range(P // kch):
        mk = _dot(sel8[:, ci * 128:(ci + 1) * 128], exp_ref[...])
        sm = jnp.where(mk > 0.5, s_ref[:, ci * kch:(ci + 1) * kch], NEG_INF)
        s_ref[:, ci * kch:(ci + 1) * kch] = sm
        mx = jnp.maximum(mx, jnp.max(sm, axis=1, keepdims=True))
    for h in range(H):
        kv = h // NSA_GROUP
        sn_ref[h:h + 1, :] = jnp.sum(newb_ref[0, 256 + kv * 64:256 + (kv + 1) * 64, :] * qb_ref[0, h],
                                     axis=0, keepdims=True)
    s_new = sn_ref[...]
    m = jnp.maximum(mx, s_new)

    def pbody(k, lsum):
        off = pl.multiple_of(k * 128, 128)
        p = jnp.exp(s_ref[:, pl.ds(off, 128)] - m)
        s_ref[:, pl.ds(off, 128)] = p
        return lsum + p

    lsum = lax.fori_loop(0, NP, pbody, jnp.zeros((H, 128), F32))
    p_new = jnp.exp(s_new - m)
    denom = jnp.sum(lsum, axis=1, keepdims=True) + p_new[:, 0:1]

    acc_ref[...] = jnp.zeros_like(acc_ref)

    def body2(i, c):
        slot = i % _NBUF
        copy2(i, slot).wait()
        prefetch(i + _NBUF - 1)
        off = pl.multiple_of((i - NP) * 128, 128)
        for h in range(H):
            acc_ref[h] = acc_ref[h] + ring[slot, 0, h // NSA_GROUP] * s_ref[h:h + 1, pl.ds(off, 128)]
        return c

    lax.fori_loop(NP, 2 * NP, body2, 0)

    zpad = jnp.zeros((HEAD_DIM, 128), F32)
    lane5 = lax.broadcasted_iota(jnp.int32, (8, wb), 1)
    lanew = lax.broadcasted_iota(jnp.int32, (HEAD_DIM, wb), 1)
    zkw = jnp.zeros((HEAD_DIM, wb), BF16)
    for kv in range(NSA_KV_HEADS):
        o_s = jnp.zeros((8, 128), F32)
        for g in range(NSA_GROUP):
            h = kv * NSA_GROUP + g
            r = _lane_sum_rows(jnp.concatenate([acc_ref[h], zpad], axis=0))
            r = (r + p_new[h:h + 1, 0:1] * vnr_ref[0, kv, 0]) / denom[h:h + 1]
            o_s = jnp.where(rowo == g, r, o_s)
        qr = qrow_ref[0, kv]
        kw = swin_ref[0, 0, 0, kv]
        vw = swin_ref[0, 0, 1, kv]
        sw = _dot(qr, jnp.concatenate([kw.astype(BF16), zkw], axis=0))
        valid_w = (wb - lane5) < WINDOW
        sw = jnp.where(valid_w, sw, NEG_INF)
        for g in range(NSA_GROUP):
            h = kv * NSA_GROUP + g
            sn_ref[g:g + 1, :] = jnp.sum(newb_ref[0, 512 + kv * 64:512 + (kv + 1) * 64, :] * qb_ref[0, h],
                                         axis=0, keepdims=True)
        s_wn = sn_ref[:, 0:1]
        m_w = jnp.maximum(jnp.max(sw, axis=1, keepdims=True), s_wn)
        p_w = jnp.where(valid_w, jnp.exp(sw - m_w), 0.0)
        p_wn = jnp.exp(s_wn - m_w)
        l_w = jnp.sum(p_w, axis=1, keepdims=True) + p_wn
        o_w = (_nt(p_w.astype(BF16), jnp.concatenate([vw.astype(BF16), zkw], axis=0))
               + p_wn * vnr_ref[0, kv, 1]) / l_w
        o_ref[0, kv] = (gate_ref[0, kv, 0] * o_cs[kv] + gate_ref[0, kv, 1] * o_s
                        + gate_ref[0, kv, 2] * o_w)
        for c in range(2):
            col = newb_ref[0, 512 + c * 128 + kv * 64:512 + c * 128 + (kv + 1) * 64, 0:1]
            rolled = pltpu.roll(swin_ref[0, 0, c, kv], wb - 1, 1)
            nwin_ref[0, c, kv] = jnp.where(lanew == wb - 1, col, rolled)


def _nsa_sample(layer, page_table, cnT, qb, qrow, newb, newrow, vnr, swT, gates, CS):
    DB, NP = page_table.shape
    P = NP * PAGE_SIZE
    wb = swT.shape[-1]
    NBP = -(-(P // CMP_BLOCK + 1) // 128) * 128
    assert NP % 4 == 0
    per_seq = lambda *shape: pl.BlockSpec((1,) + shape, lambda b, pt: (b,) + (0,) * len(shape))
    const = lambda a: pl.BlockSpec(a.shape, lambda b, pt: (0,) * a.ndim)
    consts = [CS["avg8"], CS["wbk"], CS["wbv"], CS["pem"], CS["expand"]]
    grid_spec = pltpu.PrefetchScalarGridSpec(
        num_scalar_prefetch=1, grid=(DB,),
        in_specs=[pl.BlockSpec(memory_space=pl.ANY),
                  per_seq(8, 64, 128), per_seq(2, 8, 128), per_seq(768, 128), per_seq(8, 256),
                  per_seq(2, 2, 8, 128),
                  pl.BlockSpec((1, 1, 2, 2, 64, wb), lambda b, pt: (layer, b, 0, 0, 0, 0)),
                  per_seq(2, 3, 8, 128)] + [const(a) for a in consts],
        out_specs=[per_seq(2, 8, 128), per_seq(2, 2, 64, wb)],
        scratch_shapes=[pltpu.VMEM((_NBUF, 3, 2, 64, 128), F32), pltpu.VMEM((8, P), F32),
                        pltpu.VMEM((NBP, 256), F32), pltpu.VMEM((8, 256), F32),
                        pltpu.VMEM((8, 64, 128), F32), pltpu.VMEM((8, 128), F32),
                        pltpu.SemaphoreType.DMA((_NBUF,))])
    return pl.pallas_call(
        functools.partial(_nsa_sample_kernel, layer=layer, n_pages=NP),
        grid_spec=grid_spec,
        out_shape=[jax.ShapeDtypeStruct((DB, 2, 8, 128), F32),
                   jax.ShapeDtypeStruct((DB, 2, 2, 64, wb), F32)],
        compiler_params=_cparams(1), name="nsa_sample",
    )(page_table, cnT, qb, qrow, newb, newrow, vnr, swT, gates, *consts)


_RING = 16


def _fox_decode_kernel(pt_ref, cf_ref, clf_ref, qb_ref, kn_ref, vn_ref, lfn_ref, o_ref,
                       ring, s_ref, lf_ref, w_ref, t_ref, acc_ref, sn_ref, sem, lsem, *, layer, n_pages):
    b = pl.program_id(0)
    n_seq = pl.num_programs(0)
    NP = n_pages
    H = FOX_HEADS

    def page_copy(seq, i, slot):
        c = i // NP
        pg = i - c * NP
        return pltpu.make_async_copy(cf_ref.at[layer, pt_ref[seq, pg], c], ring.at[slot], sem.at[slot])

    def logf_copy(pg):
        return pltpu.make_async_copy(clf_ref.at[layer, pt_ref[b, pg]], lf_ref.at[pg], lsem)

    def start_logf(pg, c):
        logf_copy(pg).start()
        return c

    lax.fori_loop(0, NP, start_logf, 0)

    @pl.when(b == 0)
    def _():
        for s in range(_RING - 2):
            page_copy(b, s, s).start()

    def advance_pair(i):
        slots = [(i + u) % _RING for u in range(2)]
        for u in range(2):
            page_copy(b, i + u, slots[u]).wait()
        for u in range(2):
            nxt = i + u + _RING - 2

            @pl.when(nxt < 2 * NP)
            def _():
                page_copy(b, nxt, nxt % _RING).start()

            @pl.when((nxt >= 2 * NP) & (b + 1 < n_seq))
            def _():
                page_copy(b + 1, nxt - 2 * NP, nxt % _RING).start()

        return slots

    def kbody(it, c):
        i = it * 2
        slots = advance_pair(i)
        for h in range(H):
            qh = qb_ref[0, h]
            for u in range(2):
                s_ref[i + u, h:h + 1, :] = jnp.sum(ring[slots[u], h] * qh, axis=0, keepdims=True)
        return c

    lax.fori_loop(0, NP // 2, kbody, 0)

    def wait_logf(pg, c):
        logf_copy(pg).wait()
        return c

    lax.fori_loop(0, NP, wait_logf, 0)

    ii = lax.broadcasted_iota(jnp.int32, (128, 128), 0)
    jj = lax.broadcasted_iota(jnp.int32, (128, 128), 1)
    later = jnp.where(ii > jj, 1.0, 0.0).astype(BF16)
    ones = jnp.ones((128, 128), BF16)
    l1, l2, l3 = _split3(lf_ref[...].reshape(NP * 8, 128))
    w_ref[...] = (_dot(l1, later) + _dot(l2, later) + _dot(l3, later)).reshape(NP, 8, 128)
    t_ref[...] = (_dot(l1, ones) + _dot(l2, ones) + _dot(l3, ones)).reshape(NP, 8, 128)
    for h in range(H):
        sn_ref[h:h + 1, :] = jnp.sum(kn_ref[0, h] * qb_ref[0, h], axis=0, keepdims=True)
    cn = lfn_ref[0]
    s_new = sn_ref[...] + cn - cn

    def sbody(k, carry):
        tot, mx = carry
        pg = NP - 1 - k
        sc = s_ref[pg] + cn + w_ref[pg] + tot
        s_ref[pg] = sc
        return tot + t_ref[pg], jnp.maximum(mx, sc)

    _, mx = lax.fori_loop(0, NP, sbody, (jnp.zeros((H, 128), F32), jnp.full((H, 128), NEG_INF, F32)))
    m = jnp.maximum(jnp.max(mx, axis=1, keepdims=True), s_new)
    p = jnp.exp(s_ref[...] - m[None])
    s_ref[...] = p
    p_new = jnp.exp(s_new - m)
    denom = jnp.sum(jnp.sum(p, axis=0), axis=1, keepdims=True) + p_new[:, 0:1]

    acc_ref[...] = jnp.zeros_like(acc_ref)

    def vbody(it, c):
        i = NP + it * 2
        slots = advance_pair(i)
        for h in range(H):
            a = acc_ref[h]
            for u in range(2):
                a = a + ring[slots[u], h] * s_ref[i - NP + u, h:h + 1, :]
            acc_ref[h] = a
        return c

    lax.fori_loop(0, NP // 2, vbody, 0)

    lane = lax.broadcasted_iota(jnp.int32, (HEAD_DIM, 128), 1)
    for h in range(H):
        a = acc_ref[h] + jnp.where(lane == 0, p_new[h:h + 1, :] * vn_ref[0, h], 0.0)
        o_ref[0, h:h + 1, :] = _lane_sum_rows(a)[0:1] / denom[h:h + 1]


def _fox_decode(layer, page_table, cfT, clfT, qb, knb, vnb, lfnb):
    DB, NP = page_table.shape
    assert NP % 2 == 0 and (2 * NP) % _RING == 0
    per_seq = lambda *shape: pl.BlockSpec((1,) + shape, lambda b, pt: (b,) + (0,) * len(shape))
    page_f32 = pltpu.VMEM((NP, 8, 128), F32)
    grid_spec = pltpu.PrefetchScalarGridSpec(
        num_scalar_prefetch=1, grid=(DB,),
        in_specs=[pl.BlockSpec(memory_space=pl.ANY), pl.BlockSpec(memory_space=pl.ANY),
                  per_seq(8, 64, 128), per_seq(8, 64, 128), per_seq(8, 64, 128), per_seq(8, 128)],
        out_specs=per_seq(8, 64),
        scratch_shapes=[pltpu.VMEM((_RING, 8, 64, 128), F32), page_f32, page_f32, page_f32, page_f32,
                        pltpu.VMEM((8, 64, 128), F32), pltpu.VMEM((8, 128), F32),
                        pltpu.SemaphoreType.DMA((_RING,)), pltpu.SemaphoreType.DMA(())])
    return pl.pallas_call(
        functools.partial(_fox_decode_kernel, layer=layer, n_pages=NP),
        grid_spec=grid_spec, out_shape=jax.ShapeDtypeStruct((DB, 8, 64), F32),
        compiler_params=_cparams(1), name="fox_sample",
    )(page_table, cfT, clfT, qb, knb, vnb, lfnb)


def _nsa_decode_kernel(pt_ref, cn_ref, qb_ref, qrow_ref, newb_ref, newrow_ref, vnr_ref, swin_ref,
                       gate_ref, avg8_ref, wbk_ref, wbv_ref, pem_ref, exp_ref,
                       o_ref, nwin_ref,
                       ring, s_ref, m_ref, acc_ref, sn_ref, sem, *, layer, n_pages):
    b = pl.program_id(0)
    NP = n_pages
    G = NP // 4
    P = NP * PAGE_SIZE
    nbp = P // CMP_BLOCK
    NBP = m_ref.shape[0]
    H = NSA_HEADS
    wb = swin_ref.shape[-1]
    kch = exp_ref.shape[1]

    n_seq = pl.num_programs(0)
    n_grp = _RING // 4
    ahead = n_grp

    def copy1(seq, pg, slot):
        return pltpu.make_async_copy(cn_ref.at[layer, pt_ref[seq, pg], pl.ds(0, 3)], ring.at[slot], sem.at[slot])

    def copy2(seq, pg, slot):
        return pltpu.make_async_copy(cn_ref.at[layer, pt_ref[seq, pg], 3], ring.at[slot, 0], sem.at[slot])

    def start_group(seq, gi):
        base = (gi % n_grp) * 4

        @pl.when(gi < G)
        def _():
            for u in range(4):
                copy1(seq, gi * 4 + u, base + u).start()

        @pl.when(gi >= G)
        def _():
            for u in range(4):
                copy2(seq, (gi - G) * 4 + u, base + u).start()

    def start_ahead(g):
        nxt = g + ahead

        @pl.when(nxt < 2 * G)
        def _():
            start_group(b, nxt)

        @pl.when((nxt >= 2 * G) & (b + 1 < n_seq))
        def _():
            start_group(b + 1, nxt - 2 * G)

    @pl.when(b == 0)
    def _():
        for gi in range(ahead):
            start_group(b, jnp.int32(gi))

    m_ref[...] = jnp.zeros_like(m_ref)

    def body1(g, c):
        base = (g % n_grp) * 4
        contrib = jnp.zeros((8, 256), F32)
        for u in range(4):
            copy1(b, g * 4 + u, base + u).wait()
            x1, x2 = _split2(ring[base + u, 0:2].reshape(256, 128))
            contrib = contrib + _nt(avg8_ref[u], x1) + _nt(avg8_ref[u], x2)
        m_ref[pl.ds(pl.multiple_of(g * 8, 8), 8), :] = contrib
        for h in range(H):
            qh = qb_ref[0, h]
            for u in range(4):
                off = pl.multiple_of(g * 512 + u * 128, 128)
                s_ref[h:h + 1, pl.ds(off, 128)] = jnp.sum(ring[base + u, 2, h // NSA_GROUP] * qh,
                                                          axis=0, keepdims=True)
        start_ahead(g)
        return c

    lax.fori_loop(0, G, body1, 0)

    rowm = lax.broadcasted_iota(jnp.int32, (NBP, 1), 0)
    pem = pem_ref[...]
    mm = m_ref[...] + jnp.where(rowm < nbp, pem, 0.0)
    mm = jnp.where(rowm == nbp, newrow_ref[0, 0:1, :] * (1.0 / CMP_BLOCK) + pem, mm)
    mb = mm.astype(BF16)
    n8 = lax.broadcasted_iota(jnp.int32, (8, NBP), 1)
    row8 = lax.broadcasted_iota(jnp.int32, (8, NBP), 0)
    rowo = lax.broadcasted_iota(jnp.int32, (8, 128), 0)
    cur = nbp
    kcb = [_dot(mb, wbk_ref[kv]).astype(BF16) for kv in range(NSA_KV_HEADS)]
    vcb = [_dot(mb, wbv_ref[kv]).astype(BF16) for kv in range(NSA_KV_HEADS)]
    sc = _nt(qrow_ref[0, 0], kcb[0]) + _nt(qrow_ref[0, 1], kcb[1])
    valid = (n8 + 1) * CMP_BLOCK - 1 <= P
    s = jnp.where(valid, sc, NEG_INF)
    e = jnp.exp(s - jnp.max(s, axis=1, keepdims=True))
    p_c = jnp.where(valid, e / jnp.sum(e, axis=1, keepdims=True), 0.0)
    o_c = (_dot(jnp.where(row8 < NSA_GROUP, p_c, 0.0).astype(BF16), vcb[0])
           + _dot(jnp.where(row8 >= NSA_GROUP, p_c, 0.0).astype(BF16), vcb[1]))
    imp0 = p_c[0:1] + p_c[1:2] + p_c[2:3] + p_c[3:4]
    imp1 = p_c[4:5] + p_c[5:6] + p_c[6:7] + p_c[7:8]
    n1 = n8[0:1]
    rowp = lax.broadcasted_iota(jnp.int32, (128, NBP), 0)
    m_idx = lax.broadcasted_iota(jnp.int32, (NBP, NBP), 0)
    n_idx = lax.broadcasted_iota(jnp.int32, (NBP, NBP), 1)
    imps = []
    for imp in (imp0, imp1):
        imp = jnp.where((n1 == cur) | (n1 == 0), FORCED, imp)
        imps.append(jnp.where(n1 > cur, -2.0, imp))
    impT = jnp.where(rowp == 0, imps[0], jnp.where(rowp == 1, imps[1], 0.0)).T
    sels = []
    for kv in range(NSA_KV_HEADS):
        col = impT[:, kv:kv + 1]
        beats = (col > imps[kv]) | ((col == imps[kv]) & (m_idx < n_idx))
        rank = jnp.sum(jnp.where(beats, 1.0, 0.0), axis=0, keepdims=True)
        sels.append(jnp.where((rank < N_SELECT) & (n1 <= cur), 1.0, 0.0))
    sel8 = jnp.where(row8 < NSA_GROUP, sels[0], sels[1]).astype(BF16)

    mx = jnp.full((H, 1), NEG_INF, F32)
    for ci in range(P // kch):
        mk = _dot(sel8[:, ci * 128:(ci + 1) * 128], exp_ref[...])
        sm = jnp.where(mk > 0.5, s_ref[:, ci * kch:(ci + 1) * kch], NEG_INF)
        s_ref[:, ci * kch:(ci + 1) * kch] = sm
        mx = jnp.maximum(mx, jnp.max(sm, axis=1, keepdims=True))
    for h in range(H):
        kv = h // NSA_GROUP
        sn_ref[h:h + 1, :] = jnp.sum(newb_ref[0, 256 + kv * 64:256 + (kv + 1) * 64, :] * qb_ref[0, h],
                                     axis=0, keepdims=True)
    s_new = sn_ref[...]
    m = jnp.maximum(mx, s_new)
    lsum = jnp.zeros((H, 1), F32)
    for ci in range(P // kch):
        p = jnp.exp(s_ref[:, ci * kch:(ci + 1) * kch] - m[:, 0:1])
        s_ref[:, ci * kch:(ci + 1) * kch] = p
        lsum = lsum + jnp.sum(p, axis=1, keepdims=True)
    p_new = jnp.exp(s_new - m)
    denom = lsum + p_new[:, 0:1]

    acc_ref[...] = jnp.zeros_like(acc_ref)

    def body2(g, c):
        base = (g % n_grp) * 4
        for u in range(4):
            copy2(b, (g - G) * 4 + u, base + u).wait()
        for h in range(H):
            a = acc_ref[h]
            for u in range(4):
                off = pl.multiple_of((g - G) * 512 + u * 128, 128)
                a = a + ring[base + u, 0, h // NSA_GROUP] * s_ref[h:h + 1, pl.ds(off, 128)]
            acc_ref[h] = a
        start_ahead(g)
        return c

    lax.fori_loop(G, 2 * G, body2, 0)

    zpad = jnp.zeros((HEAD_DIM, 128), F32)
    o_s = jnp.zeros((8, 128), F32)
    for h in range(H):
        r = _lane_sum_rows(jnp.concatenate([acc_ref[h], zpad], axis=0))
        o_s = jnp.where(rowo == h, r, o_s)
    o_s = (o_s + p_new[:, 0:1] * vnr_ref[0, 0]) / denom

    lane5 = lax.broadcasted_iota(jnp.int32, (8, wb), 1)
    row5 = lax.broadcasted_iota(jnp.int32, (8, wb), 0)
    zkw = jnp.zeros((HEAD_DIM, wb), BF16)
    sw = jnp.zeros((8, wb), F32)
    for kv in range(NSA_KV_HEADS):
        sw = sw + _dot(qrow_ref[0, kv], jnp.concatenate([swin_ref[0, 0, 0, kv].astype(BF16), zkw], axis=0))
    valid_w = (wb - lane5) < WINDOW
    sw = jnp.where(valid_w, sw, NEG_INF)
    for h in range(H):
        kv = h // NSA_GROUP
        sn_ref[h:h + 1, :] = jnp.sum(newb_ref[0, 512 + kv * 64:512 + (kv + 1) * 64, :] * qb_ref[0, h],
                                     axis=0, keepdims=True)
    s_wn = sn_ref[:, 0:1]
    m_w = jnp.maximum(jnp.max(sw, axis=1, keepdims=True), s_wn)
    p_w = jnp.where(valid_w, jnp.exp(sw - m_w), 0.0)
    p_wn = jnp.exp(s_wn - m_w)
    l_w = jnp.sum(p_w, axis=1, keepdims=True) + p_wn
    o_w = p_wn * vnr_ref[0, 1]
    for kv in range(NSA_KV_HEADS):
        mine = (row5 < NSA_GROUP) if kv == 0 else (row5 >= NSA_GROUP)
        o_w = o_w + _nt(jnp.where(mine, p_w, 0.0).astype(BF16),
                        jnp.concatenate([swin_ref[0, 0, 1, kv].astype(BF16), zkw], axis=0))
    o_w = o_w / l_w
    o_ref[0] = gate_ref[0, 0] * o_c + gate_ref[0, 1] * o_s + gate_ref[0, 2] * o_w

    lanew = lax.broadcasted_iota(jnp.int32, (HEAD_DIM, wb), 1)
    for kv in range(NSA_KV_HEADS):
        for c in range(2):
            col = newb_ref[0, 512 + c * 128 + kv * 64:512 + c * 128 + (kv + 1) * 64, 0:1]
            rolled = pltpu.roll(swin_ref[0, 0, c, kv], wb - 1, 1)
            nwin_ref[0, c, kv] = jnp.where(lanew == wb - 1, col, rolled)


def _nsa_decode(layer, page_table, cnT, qb, qrow, newb, newrow, vnr, swT, gates, CS):
    DB, NP = page_table.shape
    P = NP * PAGE_SIZE
    wb = swT.shape[-1]
    NBP = -(-(P // CMP_BLOCK + 1) // 128) * 128
    assert NP % 4 == 0 and (2 * NP) % _RING == 0
    per_seq = lambda *shape: pl.BlockSpec((1,) + shape, lambda b, pt: (b,) + (0,) * len(shape))
    const = lambda a: pl.BlockSpec(a.shape, lambda b, pt: (0,) * a.ndim)
    consts = [CS["avg8"], CS["wbk"], CS["wbv"], CS["pem"], CS["expand"]]
    grid_spec = pltpu.PrefetchScalarGridSpec(
        num_scalar_prefetch=1, grid=(DB,),
        in_specs=[pl.BlockSpec(memory_space=pl.ANY),
                  per_seq(8, 64, 128), per_seq(2, 8, 128), per_seq(768, 128), per_seq(8, 256),
                  per_seq(2, 8, 128),
                  pl.BlockSpec((1, 1, 2, 2, 64, wb), lambda b, pt: (layer, b, 0, 0, 0, 0)),
                  per_seq(3, 8, 128)] + [const(a) for a in consts],
        out_specs=[per_seq(8, 128), per_seq(2, 2, 64, wb)],
        scratch_shapes=[pltpu.VMEM((_RING, 3, 2, 64, 128), F32), pltpu.VMEM((8, P), F32),
                        pltpu.VMEM((NBP, 256), F32), pltpu.VMEM((8, 64, 128), F32),
                        pltpu.VMEM((8, 128), F32), pltpu.SemaphoreType.DMA((_RING,))])
    return pl.pallas_call(
        functools.partial(_nsa_decode_kernel, layer=layer, n_pages=NP),
        grid_spec=grid_spec,
        out_shape=[jax.ShapeDtypeStruct((DB, 8, 128), F32),
                   jax.ShapeDtypeStruct((DB, 2, 2, 64, wb), F32)],
        compiler_params=_cparams(1), name="nsa_sample",
    )(page_table, cnT, qb, qrow, newb, newrow, vnr, swT, gates, *consts)


def _sample_consts(P, L):
    r = jnp.arange(128)
    avg8 = jnp.stack([jnp.where(jnp.arange(8)[:, None] == 2 * p + r[None, :] // CMP_BLOCK,
                                1.0 / CMP_BLOCK, 0.0) for p in range(4)]).astype(BF16)
    w = L["w_cmp"]
    z = jnp.zeros((64, 128), F32)
    wpad = lambda c: jnp.concatenate([w[c], jnp.zeros((64, 64), F32)], axis=1)
    blocks = lambda c, kv: jnp.concatenate(
        [wpad(c) if (cc, kk) == (c, kv) else z for cc in range(2) for kk in range(2)], axis=0)
    wbk = jnp.stack([blocks(0, kv) for kv in range(2)]).astype(BF16)
    wbv = jnp.stack([blocks(1, kv) for kv in range(2)]).astype(BF16)
    pm = jnp.mean(L["pe"], axis=1)
    pem = jnp.concatenate([pm[0], pm[0], pm[1], pm[1]])[None, :]
    kch = min(P, 8192)
    expand = (r[:, None] == (jnp.arange(kch) // CMP_BLOCK)[None, :]).astype(BF16)
    return dict(avg8=avg8, wbk=wbk, wbv=wbv, pem=pem, expand=expand)


def _sample_layer(layer, xs, p_l, L, cos_s, sin_s, page_table, cnT, cfT, clfT, swT):
    DB, D = xs.shape
    P = page_table.shape[1] * PAGE_SIZE
    xpad = jnp.zeros((1, 128, D), F32).at[0, :DB].set(xs)
    nsaT, winT, foxT, logfT, qn, qf, kf, kvb, ga, gm, qT = _proj(
        xpad, L["gmix"], L["wt"], L["wg"], cos_s, sin_s, L["hg"], L["fb"], 128)
    lanes = lambda a: jnp.broadcast_to(a[..., None], a.shape + (128,))
    q_all = (qT[0, :, :DB] * SCALE).T
    qb_a = lanes(q_all[:, :512].reshape(DB, 8, 64))
    qb_f = lanes(q_all[:, 512:].reshape(DB, 8, 64))
    new = jnp.concatenate([nsaT[0], winT[0]], axis=0)[:, :DB].T
    fx = foxT[0][:, :DB].T
    lf = logfT[0][:, :DB].T
    q8 = jnp.transpose(qn[0, :, :DB, :], (1, 0, 2))
    zq = jnp.zeros_like(q8[:, :4])
    qrow = jnp.stack([jnp.concatenate([q8[:, :4], zq], axis=1),
                      jnp.concatenate([zq, q8[:, 4:]], axis=1)], axis=1)
    g = jnp.transpose(ga[0, :, :DB, :12].reshape(2, DB, 4, 3), (1, 3, 0, 2))
    gates = lanes(g.reshape(DB, 3, 8))
    vrow = jnp.stack([new[:, 384:512].reshape(DB, 2, 64), new[:, 640:768].reshape(DB, 2, 64)], axis=1)
    vrow = jnp.repeat(vrow, NSA_GROUP, axis=2)
    vnr = jnp.concatenate([vrow, jnp.zeros_like(vrow)], axis=-1)
    newrow = jnp.broadcast_to(new[:, None, :256], (DB, 8, 256))
    CS = _sample_consts(P, L)
    o_a8, new_win = _nsa_decode(layer, page_table, cnT, qb_a, qrow, lanes(new), newrow, vnr, swT, gates, CS)
    o_a = o_a8[:, :, :64].reshape(DB, 512)
    o_b = _fox_decode(layer, page_table, cfT, clfT, qb_f, lanes(fx[:, :512].reshape(DB, 8, 64)),
                      lanes(fx[:, 512:].reshape(DB, 8, 64)), lanes(lf)).reshape(DB, 512)
    y = _tail(xs, o_a, o_b, gm[0, :DB], p_l, L["tail"], DB)
    return (y, new[:, :512].reshape(DB, 1, 4, NSA_KV_HEADS, HEAD_DIM),
            fx.reshape(DB, 1, 2, FOX_HEADS, HEAD_DIM), lf.reshape(DB, 1, FOX_HEADS),
            jnp.transpose(new_win, (0, 4, 1, 2, 3)))


def kernel(x_prompt, x_sample, p_prompt, p_sample, cache_nsa, cache_fox, cache_fox_logf, state_win, page_table, norm_mix, w_in, q_norm_nsa, k_norm_nsa, q_norm_fox, k_norm_fox, fox_fbias, cmp_pe, w_cmp, w_up_nsa, w_up_fox, w_out, norm_ffn, w_ffn_gate, w_ffn_up, w_ffn_down, norm_ple, w_ple_gate, w_ple_proj):
    depth = w_in.shape[0]
    B, T, D = x_prompt.shape
    DB, S, _ = x_sample.shape
    assert S == 1 and DB <= 128
    P = page_table.shape[1] * PAGE_SIZE
    cnT = jnp.transpose(cache_nsa, (0, 1, 3, 4, 5, 2))
    cfT = jnp.transpose(cache_fox, (0, 1, 3, 4, 5, 2))
    clfT = jnp.transpose(cache_fox_logf, (0, 1, 3, 2))
    swT = jnp.transpose(state_win, (0, 1, 3, 4, 5, 2))
    C = _prompt_consts(T)
    cos_s, sin_s = _rope_tables(jnp.full((128,), P, jnp.int32))
    tm, tq = min(PROJ_TM, T), min(FOX_TQ, T)
    xp, xs = x_prompt, x_sample.reshape(DB, D)
    outs = [[] for _ in range(8)]
    for l in range(depth):
        L = _prep_layer(l, norm_mix, w_in, q_norm_nsa, k_norm_nsa, q_norm_fox, k_norm_fox, fox_fbias,
                        cmp_pe, w_cmp, w_up_nsa, w_up_fox, w_out, norm_ffn, w_ffn_gate, w_ffn_up,
                        w_ffn_down, norm_ple, w_ple_gate, w_ple_proj)
        xp, nsa_p, fox_p, lf_p, win_p = _prompt_layer(xp, p_prompt[l], L, C, tm, tq, tm)
        xs, nsa_s, fox_s, lf_s, win_s = _sample_layer(l, xs, p_sample[l].reshape(DB, -1), L, cos_s, sin_s,
                                                      page_table, cnT, cfT, clfT, swT)
        for acc, v in zip(outs, (nsa_p, nsa_s, fox_p, fox_s, lf_p, lf_s, win_p, win_s)):
            acc.append(v)
    return (xp, xs.reshape(DB, 1, D)) + tuple(jnp.stack(o) for o in outs)
```

```python
import functools

import jax
import jax.numpy as jnp
import numpy as np
from jax import lax
from jax.experimental import pallas as pl
from jax.experimental.pallas import tpu as pltpu

HEAD_DIM = 64
NSA_HEADS = 8
NSA_KV_HEADS = 2
NSA_GROUP = NSA_HEADS // NSA_KV_HEADS
FOX_HEADS = 8
CMP_BLOCK = 64
N_SELECT = 16
WINDOW = 512
PAGE_SIZE = 128
ROPE_THETA = 10000.0
EPS = 1e-6
NEG_INF = -1e30
FORCED = 1e4
SCALE = HEAD_DIM ** -0.5

PROJ_TM = 256
TAIL_TM = 256
FOX_TQ, FOX_TK = 512, 1024
NSA_CH = 1024
LANES = 128
VMEM_LIMIT = 56 * 1024 * 1024
UNSELECTED = NEG_INF

F32 = jnp.float32
BF16 = jnp.bfloat16

_R_QA, _R_KV6, _R_QB, _R_KB, _R_VB, _R_FB, _R_END = 0, 512, 1280, 1792, 2304, 2816, 2832


def _nt(a, b):
    return lax.dot_general(a, b, (((1,), (1,)), ((), ())), preferred_element_type=F32)


def _dot(a, b):
    return jnp.dot(a, b, preferred_element_type=F32)


def _split3(x):
    a = x.astype(BF16)
    r = x - a.astype(F32)
    b = r.astype(BF16)
    c = (r - b.astype(F32)).astype(BF16)
    return a, b, c


def _log_sigmoid(x):
    return jnp.minimum(x, 0.0) - jnp.log1p(jnp.exp(-jnp.abs(x)))


def _cparams(n_grid):
    return pltpu.CompilerParams(dimension_semantics=("arbitrary",) * n_grid,
                                vmem_limit_bytes=VMEM_LIMIT)


def _proj_kernel(x_ref, gmix_ref, wt_ref, wg_ref, cos_ref, sin_ref, hg_ref, fb_ref,
                 nsaT_ref, winT_ref, foxT_ref, logfT_ref, qn_ref, qf_ref, kf_ref, kvb_ref,
                 ga_ref, gm_ref, qT_ref, carry_ref, *, tm):
    j = pl.program_id(1)

    @pl.when(j == 0)
    def _():
        carry_ref[...] = jnp.zeros_like(carry_ref)

    x = x_ref[0]
    ms = jnp.mean(x * x, axis=-1, keepdims=True)
    h = (x * lax.rsqrt(ms + EPS) * gmix_ref[...]).astype(BF16)
    cos = cos_ref[...]
    sin = sin_ref[...]
    hg = hg_ref[...]

    def zt(r0, r1):
        return _nt(wt_ref[r0:r1, :], h)

    def headnorm(z, col):
        ss = jnp.sum(z * z, axis=0, keepdims=True)
        return z * lax.rsqrt(ss * (1.0 / HEAD_DIM) + EPS) * hg[:, col:col + 1]

    def rope(z):
        x1, x2 = z[:HEAD_DIM // 2], z[HEAD_DIM // 2:]
        return jnp.concatenate([x1 * cos - x2 * sin, x2 * cos + x1 * sin], axis=0)

    zeros64 = jnp.zeros((HEAD_DIM, tm), F32)
    row8 = lax.broadcasted_iota(jnp.int32, (8, tm), 0)

    for hp in range(NSA_HEADS // 2):
        z = zt(_R_QA + hp * 128, _R_QA + (hp + 1) * 128)
        for i in range(2):
            hd = hp * 2 + i
            q = rope(headnorm(z[i * 64:(i + 1) * 64], 0))
            qT_ref[0, hd * 64:(hd + 1) * 64, :] = q
            blk = jnp.concatenate([q * SCALE, zeros64], axis=0)
            qn_ref[0, hd] = blk.T.astype(BF16)

    for part in range(6):
        z = zt(_R_KV6 + part * 128, _R_KV6 + (part + 1) * 128)
        if part % 2 == 0:
            z = jnp.concatenate([rope(headnorm(z[:64], 1 + part // 2)),
                                 rope(headnorm(z[64:], 1 + part // 2))], axis=0)
        if part < 4:
            nsaT_ref[0, part * 128:(part + 1) * 128, :] = z
        else:
            winT_ref[0, (part - 4) * 128:(part - 3) * 128, :] = z
        if part >= 2:
            kvb_ref[0, (part - 2) * 128:(part - 1) * 128, :] = z.astype(BF16)

    zf = zt(_R_FB, _R_END)[:8]
    logf = _log_sigmoid(zf + fb_ref[...])
    logfT_ref[0] = logf
    ii = lax.broadcasted_iota(jnp.int32, (tm, tm), 0)
    jj = lax.broadcasted_iota(jnp.int32, (tm, tm), 1)
    tri = jnp.where(ii <= jj, 1.0, 0.0).astype(BF16)
    l1, l2, l3 = _split3(logf)
    c = carry_ref[:, 0:1] + (_dot(l1, tri) + _dot(l2, tri) + _dot(l3, tri))
    carry_ref[...] = jnp.broadcast_to(c[:, tm - 1:tm], carry_ref.shape)
    c1, c2, c3 = [p.astype(F32) for p in _split3(c)]

    for hp in range(FOX_HEADS // 2):
        z = zt(_R_QB + hp * 128, _R_QB + (hp + 1) * 128)
        for i in range(2):
            hd = hp * 2 + i
            q = headnorm(z[i * 64:(i + 1) * 64], 4)
            qT_ref[0, 512 + hd * 64:512 + (hd + 1) * 64, :] = q
            aug = jnp.where(row8 == 0, c1[hd:hd + 1],
                            jnp.where(row8 == 1, c2[hd:hd + 1],
                                      jnp.where(row8 == 2, c3[hd:hd + 1],
                                                jnp.where(row8 < 6, 1.0, 0.0))))
            blk = jnp.concatenate([q * SCALE, aug, jnp.zeros((56, tm), F32)], axis=0)
            qf_ref[0, hd] = blk.T.astype(BF16)

    for hp in range(FOX_HEADS // 2):
        z = zt(_R_KB + hp * 128, _R_KB + (hp + 1) * 128)
        for i in range(2):
            hd = hp * 2 + i
            k = headnorm(z[i * 64:(i + 1) * 64], 5)
            foxT_ref[0, hd * 64:(hd + 1) * 64, :] = k
            aug = jnp.where(row8 < 3, 1.0,
                            jnp.where(row8 == 3, -c1[hd:hd + 1],
                                      jnp.where(row8 == 4, -c2[hd:hd + 1],
                                                jnp.where(row8 == 5, -c3[hd:hd + 1], 0.0))))
            blk = jnp.concatenate([k, aug, jnp.zeros((56, tm), F32)], axis=0)
            kf_ref[0, hd] = blk.astype(BF16)

    for hp in range(FOX_HEADS // 2):
        z = zt(_R_VB + hp * 128, _R_VB + (hp + 1) * 128)
        foxT_ref[0, 512 + hp * 128:512 + (hp + 1) * 128, :] = z
        kvb_ref[0, 512 + hp * 128:512 + (hp + 1) * 128, :] = z.astype(BF16)

    zg = _dot(h, wg_ref[:, 0:256])
    ga_ref[0, 0] = jax.nn.sigmoid(zg[:, 0:128])
    ga_ref[0, 1] = jax.nn.sigmoid(zg[:, 128:256])
    ngm = gm_ref.shape[-1]
    for c0 in range(0, ngm, 512):
        gm_ref[0, :, c0:c0 + 512] = jax.nn.sigmoid(_dot(h, wg_ref[:, 256 + c0:256 + c0 + 512]))


def _proj(x, gmix, wt, wg, cos, sin, hg, fb, tm):
    B, T, D = x.shape
    grid = (B, T // tm)
    ngm = wg.shape[1] - 256
    const = lambda *shape: pl.BlockSpec(shape, lambda b, j: (0,) * len(shape), pipeline_mode=pl.Buffered(1))
    out_shape = [
        jax.ShapeDtypeStruct((B, 512, T), F32),
        jax.ShapeDtypeStruct((B, 256, T), F32),
        jax.ShapeDtypeStruct((B, 1024, T), F32),
        jax.ShapeDtypeStruct((B, 8, T), F32),
        jax.ShapeDtypeStruct((B, NSA_HEADS, T, 128), BF16),
        jax.ShapeDtypeStruct((B, FOX_HEADS, T, 128), BF16),
        jax.ShapeDtypeStruct((B, FOX_HEADS, 128, T), BF16),
        jax.ShapeDtypeStruct((B, 1024, T), BF16),
        jax.ShapeDtypeStruct((B, 2, T, 128), F32),
        jax.ShapeDtypeStruct((B, T, ngm), F32),
        jax.ShapeDtypeStruct((B, 1024, T), F32),
    ]
    out_specs = [
        pl.BlockSpec((1, 512, tm), lambda b, j: (b, 0, j)),
        pl.BlockSpec((1, 256, tm), lambda b, j: (b, 0, j)),
        pl.BlockSpec((1, 1024, tm), lambda b, j: (b, 0, j)),
        pl.BlockSpec((1, 8, tm), lambda b, j: (b, 0, j)),
        pl.BlockSpec((1, NSA_HEADS, tm, 128), lambda b, j: (b, 0, j, 0)),
        pl.BlockSpec((1, FOX_HEADS, tm, 128), lambda b, j: (b, 0, j, 0)),
        pl.BlockSpec((1, FOX_HEADS, 128, tm), lambda b, j: (b, 0, 0, j)),
        pl.BlockSpec((1, 1024, tm), lambda b, j: (b, 0, j)),
        pl.BlockSpec((1, 2, tm, 128), lambda b, j: (b, 0, j, 0)),
        pl.BlockSpec((1, tm, ngm), lambda b, j: (b, j, 0)),
        pl.BlockSpec((1, 1024, tm), lambda b, j: (b, 0, j)),
    ]
    in_specs = [
        pl.BlockSpec((1, tm, D), lambda b, j: (b, j, 0)),
        const(1, D),
        const(*wt.shape),
        const(*wg.shape),
        pl.BlockSpec((32, tm), lambda b, j: (0, j)),
        pl.BlockSpec((32, tm), lambda b, j: (0, j)),
        const(64, 8),
        const(8, 1),
    ]
    return pl.pallas_call(
        functools.partial(_proj_kernel, tm=tm),
        grid=grid, in_specs=in_specs, out_specs=out_specs, out_shape=out_shape,
        scratch_shapes=[pltpu.VMEM((8, 128), F32)],
        compiler_params=_cparams(2), name="proj",
    )(x, gmix, wt, wg, cos, sin, hg, fb)


def _fox_kernel(q_ref, k_ref, v_ref, o_ref, *, tq, tk):
    j = pl.program_id(2)
    row = lax.broadcasted_iota(jnp.int32, (128, tk), 0)
    lane_o = lax.broadcasted_iota(jnp.int32, (tq, 128), 1)
    one_row = (64, 0)
    keep = [jnp.where((row < 64) if i == 0 else (row >= 64), 1.0, 0.0).astype(BF16) for i in range(2)]
    fill = [jnp.where(row == one_row[i], 1.0, 0.0).astype(BF16) for i in range(2)]
    qs = [q_ref[0, i] for i in range(2)]

    def step(c, carry, causal):
        off = pl.multiple_of(c * tk, tk)
        vs = v_ref[0, :, pl.ds(off, tk)]
        out = []
        for i in range(2):
            m, acc = carry[i]
            s = _dot(qs[i], k_ref[0, i, :, pl.ds(off, tk)])
            if causal:
                rr = j * tq + lax.broadcasted_iota(jnp.int32, (tq, tk), 0)
                cc = c * tk + lax.broadcasted_iota(jnp.int32, (tq, tk), 1)
                s = jnp.where(cc <= rr, s, NEG_INF)
            m_new = jnp.maximum(m, jnp.max(s, axis=1, keepdims=True))
            p = jnp.exp(s - m_new).astype(BF16)
            acc = jnp.exp(m - m_new) * acc + _nt(p, vs * keep[i] + fill[i])
            out.append((m_new, acc))
        return tuple(out)

    init = tuple((jnp.full((tq, 1), NEG_INF, F32), jnp.zeros((tq, 128), F32)) for _ in range(2))
    last = (j * tq) // tk
    carry = lax.fori_loop(0, last, functools.partial(step, causal=False), init)
    (_, a0), (_, a1) = step(last, carry, True)
    o_ref[0] = jnp.where(lane_o < 64, a0 / a0[:, 64:65], a1 / a1[:, 0:1])


def _fox_prompt(qf, kf, kvb, tq):
    B, H, T, _ = qf.shape
    grid = (B, H // 2, T // tq)
    tk = min(FOX_TK, T)
    assert T % tk == 0 and tk % tq == 0
    return pl.pallas_call(
        functools.partial(_fox_kernel, tq=tq, tk=tk),
        grid=grid,
        in_specs=[
            pl.BlockSpec((1, 2, tq, 128), lambda b, hp, j: (b, hp, j, 0)),
            pl.BlockSpec((1, 2, 128, T), lambda b, hp, j: (b, hp, 0, 0)),
            pl.BlockSpec((1, 128, T), lambda b, hp, j: (b, 4 + hp, 0)),
        ],
        out_specs=pl.BlockSpec((1, tq, 128), lambda b, hp, j: (b, j, hp)),
        out_shape=jax.ShapeDtypeStruct((B, T, 512), F32),
        compiler_params=_cparams(3), name="fox_prompt",
    )(qf, kf, kvb)


def _cmp_kernel(xk_ref, xv_ref, avg_ref, avgT_ref, peT_ref, pe_ref, w0T_ref, w1_ref,
                kcb_ref, vcb_ref):
    lane = lax.broadcasted_iota(jnp.int32, (128, 128), 1)
    rowi = lax.broadcasted_iota(jnp.int32, (128, 128), 0)
    xk = xk_ref[0]
    k1, k2, _ = _split3(xk)
    mk = _dot(k1, avg_ref[...]) + _dot(k2, avg_ref[...])
    mk = mk + jnp.mean(peT_ref[0], axis=1, keepdims=True)
    kcb = _dot(w0T_ref[...], mk.astype(BF16))
    kcb_ref[0, 0] = jnp.where(lane >= 64, kcb, 0.0).astype(BF16)
    xv = xv_ref[0]
    v1, v2, _ = _split3(xv)
    mv = _nt(avgT_ref[...], v1) + _nt(avgT_ref[...], v2)
    mv = mv + jnp.mean(pe_ref[1], axis=0, keepdims=True)
    vcb = _dot(mv.astype(BF16), w1_ref[...])
    vcb_ref[0, 0] = jnp.where(rowi >= 64, vcb, 0.0).astype(BF16)


def _cmp_prompt(nsaT, avg, avgT, peT, pe, w0T, w1):
    B, _, T = nsaT.shape
    const = lambda *shape: pl.BlockSpec(shape, lambda b, kv: (0,) * len(shape))
    return pl.pallas_call(
        _cmp_kernel,
        grid=(B, NSA_KV_HEADS),
        in_specs=[
            pl.BlockSpec((1, 64, T), lambda b, kv: (b, kv, 0)),
            pl.BlockSpec((1, 64, T), lambda b, kv: (b, 2 + kv, 0)),
            const(T, 128), const(128, T), const(2, 64, 64), const(2, 64, 64),
            const(128, 64), const(64, 128),
        ],
        out_specs=[pl.BlockSpec((1, 1, 128, 128), lambda b, kv: (b, kv, 0, 0))] * 2,
        out_shape=[jax.ShapeDtypeStruct((B, NSA_KV_HEADS, 128, 128), BF16)] * 2,
        compiler_params=_cparams(2), name="cmp_prompt",
    )(nsaT, nsaT, avg, avgT, peT, pe, w0T, w1)


def _nsa_kernel(q_ref, kcb_ref, vcb_ref, ks_ref, vs_ref, kw_ref, vw_ref, oh_ref, ga_ref,
                o_ref, *, nb, ch, ww, n_tiles):
    j = pl.program_id(2)
    QB = 128
    R = NSA_GROUP * QB
    q4 = q_ref[0].reshape(R, 128)
    lane = lax.broadcasted_iota(jnp.int32, (QB, 128), 1)
    n_idx = lane - 64
    r_q = lax.broadcasted_iota(jnp.int32, (QB, 128), 0)
    t = j * QB + r_q
    real = (lane >= 64) & (n_idx < nb)
    rep4 = lambda a: jnp.concatenate([a] * NSA_GROUP, axis=0)
    lane4 = lax.broadcasted_iota(jnp.int32, (R, 128), 1)
    t4 = j * QB + lax.broadcasted_iota(jnp.int32, (R, 128), 0) % QB

    s_c = _dot(q4, kcb_ref[0, 0])
    valid_c = (lane4 >= 64) & (lane4 - 64 < nb) & ((lane4 - 63) * CMP_BLOCK - 1 <= t4)
    s = jnp.where(valid_c, s_c, NEG_INF)
    e = jnp.exp(s - jnp.max(s, axis=1, keepdims=True))
    p_c = jnp.where(valid_c, e / jnp.sum(e, axis=1, keepdims=True), 0.0)
    o_c = _dot(p_c.astype(BF16), vcb_ref[0, 0])

    imp = p_c[0:QB] + p_c[QB:2 * QB] + p_c[2 * QB:3 * QB] + p_c[3 * QB:4 * QB]
    cur = t // CMP_BLOCK
    imp = jnp.where((n_idx == cur) | (n_idx == 0), FORCED, jnp.where(n_idx > cur, -1.0, imp))
    imp = jnp.where(real, imp, -2.0)
    impT = imp.T[64:128]
    n_row = lax.broadcasted_iota(jnp.int32, (64, QB), 0)
    curT = (j * QB + lax.broadcasted_iota(jnp.int32, (64, QB), 1)) // CMP_BLOCK
    rankT = jnp.zeros((64, QB), jnp.int32)
    for mm in range(nb):
        rowm = impT[mm:mm + 1, :]
        beats = (rowm > impT) | ((rowm == impT) & (n_row > mm))
        rankT = rankT + beats.astype(jnp.int32)
    selT = (rankT < N_SELECT) & (n_row <= curT) & (n_row < nb)
    biasT = jnp.where(selT, 0.0, UNSELECTED)
    bias = jnp.concatenate([jnp.zeros((64, QB), F32), biasT], axis=0).T
    q_sel = (q4.astype(F32) + rep4(bias)).astype(BF16)

    def ones_rows(width):
        return jnp.where(lax.broadcasted_iota(jnp.int32, (64, width), 0) == 0, 1.0, 0.0).astype(BF16)

    def attend(q, k_ref_, v_ref_, off, width, bias_q, carry):
        m, acc = carry
        k_aug = jnp.concatenate([k_ref_[0, :, pl.ds(off, width)], oh_ref[:, pl.ds(off, width)]], axis=0)
        s = _dot(q, k_aug)
        if bias_q is not None:
            s = s + rep4(bias_q)
        m_new = jnp.maximum(m, jnp.max(s, axis=1, keepdims=True))
        p = jnp.exp(s - m_new).astype(BF16)
        v_aug = jnp.concatenate([v_ref_[0, :, pl.ds(off, width)], ones_rows(width)], axis=0)
        acc = jnp.exp(m - m_new) * acc + _nt(p, v_aug)
        return m_new, acc

    init = (jnp.full((R, 1), NEG_INF, F32), jnp.zeros((R, 128), F32))

    last = (j * QB) // ch
    carry = lax.fori_loop(
        0, last, lambda c, cr: attend(q_sel, ks_ref, vs_ref, pl.multiple_of(c * ch, ch), ch, None, cr), init)
    tq_c = j * QB + lax.broadcasted_iota(jnp.int32, (QB, ch), 0)
    kp_c = last * ch + lax.broadcasted_iota(jnp.int32, (QB, ch), 1)
    _, acc_s = attend(q_sel, ks_ref, vs_ref, pl.multiple_of(last * ch, ch), ch,
                      jnp.where(kp_c <= tq_c, 0.0, NEG_INF), carry)
    o_s = acc_s / acc_s[:, 64:65]

    ws = jnp.clip(j - WINDOW // QB, 0, n_tiles - ww // QB)
    dpos = (j - ws) * QB + lax.broadcasted_iota(jnp.int32, (QB, ww), 0) \
        - lax.broadcasted_iota(jnp.int32, (QB, ww), 1)
    _, acc_w = attend(q4, kw_ref, vw_ref, pl.multiple_of(ws * QB, QB), ww,
                      jnp.where((dpos >= 0) & (dpos < WINDOW), 0.0, NEG_INF), init)
    o_w = acc_w / acc_w[:, 64:65]

    ga = ga_ref[0, 0]
    outs = []
    for g in range(NSA_GROUP):
        sl = slice(g * QB, (g + 1) * QB)
        outs.append(ga[:, 3 * g:3 * g + 1] * o_c[sl] + ga[:, 3 * g + 1:3 * g + 2] * o_s[sl]
                    + ga[:, 3 * g + 2:3 * g + 3] * o_w[sl])
    lo = lane < 64
    o_ref[0, :, 0:128] = jnp.where(lo, outs[0], pltpu.roll(outs[1], 64, 1))
    o_ref[0, :, 128:256] = jnp.where(lo, outs[2], pltpu.roll(outs[3], 64, 1))


def _nsa_prompt(qn, kcb, vcb, kvb, onehot, ga):
    B, H, T, _ = qn.shape
    nb = T // CMP_BLOCK
    assert nb <= 64
    QB = 128
    grid = (B, NSA_KV_HEADS, T // QB)
    kvspec = lambda base: pl.BlockSpec((1, 64, T), lambda b, kv, j: (b, base + kv, 0))
    return pl.pallas_call(
        functools.partial(_nsa_kernel, nb=nb, ch=min(NSA_CH, T), ww=min(WINDOW + QB, T), n_tiles=T // QB),
        grid=grid,
        in_specs=[
            pl.BlockSpec((1, NSA_GROUP, QB, 128), lambda b, kv, j: (b, kv, j, 0)),
            pl.BlockSpec((1, 1, 128, 128), lambda b, kv, j: (b, kv, 0, 0)),
            pl.BlockSpec((1, 1, 128, 128), lambda b, kv, j: (b, kv, 0, 0)),
            kvspec(0), kvspec(2), kvspec(4), kvspec(6),
            pl.BlockSpec((64, T), lambda b, kv, j: (0, 0)),
            pl.BlockSpec((1, 1, QB, 128), lambda b, kv, j: (b, kv, j, 0)),
        ],
        out_specs=pl.BlockSpec((1, QB, 256), lambda b, kv, j: (b, j, kv)),
        out_shape=jax.ShapeDtypeStruct((B, T, 512), F32),
        compiler_params=_cparams(3), name="nsa_prompt",
    )(qn, kcb, vcb, kvb, kvb, kvb, kvb, onehot, ga)


def _tail_kernel(x_ref, oa_ref, ob_ref, gm_ref, p_ref, wua_ref, wub_ref, wo_ref, nf_ref,
                 wg_ref, wu_ref, wd_ref, npl_ref, wpg_ref, wpp_ref, y_ref):
    D = x_ref.shape[-1]
    u_a = _dot(oa_ref[...].astype(BF16), wua_ref[...])
    u_b = _dot(ob_ref[...].astype(BF16), wub_ref[...])
    mixed = gm_ref[:, 0:D] * u_a + gm_ref[:, D:2 * D] * u_b
    x1 = x_ref[...] + _dot(mixed.astype(BF16), wo_ref[...])

    def rms(v, g):
        return (v * lax.rsqrt(jnp.mean(v * v, axis=-1, keepdims=True) + EPS) * g).astype(BF16)

    h = rms(x1, nf_ref[...])
    ff = jax.nn.silu(_dot(h, wg_ref[...])) * _dot(h, wu_ref[...])
    x2 = x1 + _dot(ff.astype(BF16), wd_ref[...])
    gate = jax.nn.sigmoid(_dot(rms(x2, npl_ref[...]), wpg_ref[...]))
    y_ref[...] = x2 + gate * _dot(p_ref[...].astype(BF16), wpp_ref[...])


def _tail(x, oa, ob, gm, p, w, tm):
    N, D = x.shape
    tok = lambda width: pl.BlockSpec((tm, width), lambda i: (i, 0))
    const = lambda a: pl.BlockSpec(a.shape, lambda i: (0,) * a.ndim, pipeline_mode=pl.Buffered(1))
    return pl.pallas_call(
        _tail_kernel,
        grid=(N // tm,),
        in_specs=[tok(D), tok(oa.shape[1]), tok(ob.shape[1]), tok(gm.shape[1]), tok(p.shape[1])]
        + [const(a) for a in w],
        out_specs=tok(D),
        out_shape=jax.ShapeDtypeStruct((N, D), F32),
        compiler_params=_cparams(1), name="tail",
    )(x, oa, ob, gm, p, *w)


def _prep_layer(l, norm_mix, w_in, q_norm_nsa, k_norm_nsa, q_norm_fox, k_norm_fox, fox_fbias,
                cmp_pe, w_cmp, w_up_nsa, w_up_fox, w_out, norm_ffn, w_ffn_gate, w_ffn_up,
                w_ffn_down, norm_ple, w_ple_gate, w_ple_proj):
    wl = w_in[l]
    D = wl.shape[0]
    o_ga = 512 + 6 * 128
    o_qb = o_ga + 3 * NSA_HEADS
    o_fb = o_qb + 3 * 512
    o_gm = o_fb + FOX_HEADS
    wt = jnp.concatenate([wl[:, 0:o_ga], wl[:, o_qb:o_gm], jnp.zeros((D, 8), F32)], axis=1).T.astype(BF16)
    zpad = jnp.zeros((D, 128 - 12), F32)
    wg = jnp.concatenate([wl[:, o_ga:o_ga + 12], zpad, wl[:, o_ga + 12:o_ga + 24], zpad, wl[:, o_gm:]],
                         axis=1).astype(BF16)
    hg = jnp.stack([q_norm_nsa[l], k_norm_nsa[l, 0], k_norm_nsa[l, 1], k_norm_nsa[l, 2],
                    q_norm_fox[l], k_norm_fox[l], jnp.zeros((64,), F32), jnp.zeros((64,), F32)], axis=1)
    z64 = jnp.zeros((64, 64), F32)
    return dict(
        gmix=norm_mix[l][None, :], wt=wt, wg=wg, hg=hg, fb=fox_fbias[l][:, None],
        pe=cmp_pe[l], peT=jnp.swapaxes(cmp_pe[l], 1, 2),
        w0T=jnp.concatenate([w_cmp[l, 0].T, z64], axis=0).astype(BF16),
        w1=jnp.concatenate([w_cmp[l, 1], z64], axis=1).astype(BF16),
        w_cmp=w_cmp[l],
        tail=(w_up_nsa[l].astype(BF16), w_up_fox[l].astype(BF16), w_out[l].astype(BF16),
              norm_ffn[l][None, :], w_ffn_gate[l].astype(BF16), w_ffn_up[l].astype(BF16),
              w_ffn_down[l].astype(BF16), norm_ple[l][None, :], w_ple_gate[l].astype(BF16),
              w_ple_proj[l].astype(BF16)),
    )


def _rope_tables(pos):
    half = HEAD_DIM // 2
    inv_freq = ROPE_THETA ** (-jnp.arange(half, dtype=F32) / half)
    ang = inv_freq[:, None] * pos.astype(F32)[None, :]
    return jnp.cos(ang), jnp.sin(ang)


def _prompt_consts(T):
    tpos = jnp.arange(T, dtype=jnp.int32)
    blk = tpos // CMP_BLOCK
    avg = jnp.where(jnp.arange(128)[None, :] == 64 + blk[:, None], 1.0 / CMP_BLOCK, 0.0).astype(BF16)
    onehot = (jnp.arange(64)[:, None] == blk[None, :]).astype(BF16)
    cos, sin = _rope_tables(tpos)
    return dict(avg=avg, avgT=avg.T, onehot=onehot, cos=cos, sin=sin)


def _prompt_layer(xp, p_l, L, C, tm_proj, tq_fox, tm_tail):
    B, T, D = xp.shape
    nsaT, winT, foxT, logfT, qn, qf, kf, kvb, ga, gm, _ = _proj(
        xp, L["gmix"], L["wt"], L["wg"], C["cos"], C["sin"], L["hg"], L["fb"], tm_proj)
    kcb, vcb = _cmp_prompt(nsaT, C["avg"], C["avgT"], L["peT"], L["pe"], L["w0T"], L["w1"])
    o_a = _nsa_prompt(qn, kcb, vcb, kvb, C["onehot"], ga)
    o_b = _fox_prompt(qf, kf, kvb, tq_fox)
    N = B * T
    y = _tail(xp.reshape(N, D), o_a.reshape(N, 512), o_b.reshape(N, 512), gm.reshape(N, gm.shape[-1]),
              p_l.reshape(N, p_l.shape[-1]), L["tail"], tm_tail).reshape(B, T, D)
    nsa_rows = jnp.transpose(nsaT.reshape(B, 4, NSA_KV_HEADS, HEAD_DIM, T), (0, 4, 1, 2, 3))
    fox_rows = jnp.transpose(foxT.reshape(B, 2, FOX_HEADS, HEAD_DIM, T), (0, 4, 1, 2, 3))
    logf = jnp.transpose(logfT, (0, 2, 1))
    wk = min(WINDOW, T)
    win_rows = jnp.transpose(winT[:, :, T - wk:].reshape(B, 2, NSA_KV_HEADS, HEAD_DIM, wk), (0, 4, 1, 2, 3))
    return y, nsa_rows, fox_rows, logf, win_rows


_NBUF = 6


def _lane_sum_rows(a):
    ones = jnp.ones((8, a.shape[1]), BF16)
    a1, a2, a3 = _split3(a)
    return _nt(ones, a1) + _nt(ones, a2) + _nt(ones, a3)


def _fox_sample_kernel(pt_ref, cf_ref, clf_ref, qb_ref, kn_ref, vn_ref, lfn_ref, o_ref,
                       ring, s_ref, lf_ref, acc_ref, sn_ref, sem, lsem, *, layer, n_pages):
    b = pl.program_id(0)
    NP = n_pages
    H = FOX_HEADS

    def page_copy(i, slot):
        c = i // NP
        pg = i - c * NP
        return pltpu.make_async_copy(cf_ref.at[layer, pt_ref[b, pg], c], ring.at[slot], sem.at[slot])

    def logf_copy(pg):
        return pltpu.make_async_copy(clf_ref.at[layer, pt_ref[b, pg]],
                                     lf_ref.at[:, pl.ds(pl.multiple_of(pg * 128, 128), 128)], lsem)

    def start_logf(pg, c):
        logf_copy(pg).start()
        return c

    lax.fori_loop(0, NP, start_logf, 0)
    for s in range(_NBUF - 1):
        page_copy(s, s).start()

    def advance(i):
        slot = i % _NBUF
        page_copy(i, slot).wait()
        nxt = i + _NBUF - 1

        @pl.when(nxt < 2 * NP)
        def _():
            page_copy(nxt, nxt % _NBUF).start()

        return slot

    def kbody(i, c):
        slot = advance(i)
        off = pl.multiple_of(i * 128, 128)
        for h in range(H):
            s_ref[h:h + 1, pl.ds(off, 128)] = jnp.sum(ring[slot, h] * qb_ref[0, h], axis=0, keepdims=True)
        return c

    lax.fori_loop(0, NP, kbody, 0)

    def wait_logf(pg, c):
        logf_copy(pg).wait()
        return c

    lax.fori_loop(0, NP, wait_logf, 0)

    for h in range(H):
        sn_ref[h:h + 1, :] = jnp.sum(kn_ref[0, h] * qb_ref[0, h], axis=0, keepdims=True)
    s_new = sn_ref[...]
    cn = lfn_ref[0]
    ii = lax.broadcasted_iota(jnp.int32, (128, 128), 0)
    jj = lax.broadcasted_iota(jnp.int32, (128, 128), 1)
    later = jnp.where(ii > jj, 1.0, 0.0).astype(BF16)

    def sbody(k, carry):
        tot, mx = carry
        off = pl.multiple_of((NP - 1 - k) * 128, 128)
        chunk = lf_ref[:, pl.ds(off, 128)]
        l1, l2, l3 = _split3(chunk)
        within = _dot(l1, later) + _dot(l2, later) + _dot(l3, later)
        sc = s_ref[:, pl.ds(off, 128)] + cn + within + tot
        s_ref[:, pl.ds(off, 128)] = sc
        return tot + jnp.sum(chunk, axis=1, keepdims=True), jnp.maximum(mx, jnp.max(sc, axis=1, keepdims=True))

    _, mx = lax.fori_loop(0, NP, sbody, (jnp.zeros((H, 1), F32), jnp.full((H, 1), NEG_INF, F32)))
    m = jnp.maximum(mx, s_new + cn - cn)

    def pbody(k, lsum):
        off = pl.multiple_of(k * 128, 128)
        p = jnp.exp(s_ref[:, pl.ds(off, 128)] - m)
        s_ref[:, pl.ds(off, 128)] = p
        return lsum + p

    lsum = lax.fori_loop(0, NP, pbody, jnp.zeros((H, 128), F32))
    p_new = jnp.exp(s_new + cn - cn - m)
    denom = jnp.sum(lsum, axis=1, keepdims=True) + p_new[:, 0:1]

    acc_ref[...] = jnp.zeros_like(acc_ref)

    def vbody(i, c):
        slot = advance(i)
        off = pl.multiple_of((i - NP) * 128, 128)
        for h in range(H):
            acc_ref[h] = acc_ref[h] + ring[slot, h] * s_ref[h:h + 1, pl.ds(off, 128)]
        return c

    lax.fori_loop(NP, 2 * NP, vbody, 0)

    lane = lax.broadcasted_iota(jnp.int32, (HEAD_DIM, 128), 1)
    for h in range(H):
        a = acc_ref[h] + jnp.where(lane == 0, p_new[h:h + 1, :] * vn_ref[0, h], 0.0)
        o_ref[0, h:h + 1, :] = _lane_sum_rows(a)[0:1] / denom[h:h + 1]


def _fox_sample(layer, page_table, cfT, clfT, qb, knb, vnb, lfnb):
    DB, NP = page_table.shape
    P = NP * PAGE_SIZE
    per_seq = lambda *shape: pl.BlockSpec((1,) + shape, lambda b, pt: (b,) + (0,) * len(shape))
    grid_spec = pltpu.PrefetchScalarGridSpec(
        num_scalar_prefetch=1, grid=(DB,),
        in_specs=[pl.BlockSpec(memory_space=pl.ANY), pl.BlockSpec(memory_space=pl.ANY),
                  per_seq(8, 64, 128), per_seq(8, 64, 128), per_seq(8, 64, 128), per_seq(8, 128)],
        out_specs=per_seq(8, 64),
        scratch_shapes=[pltpu.VMEM((_NBUF, 8, 64, 128), F32), pltpu.VMEM((8, P), F32),
                        pltpu.VMEM((8, P), F32), pltpu.VMEM((8, 64, 128), F32), pltpu.VMEM((8, 128), F32),
                        pltpu.SemaphoreType.DMA((_NBUF,)), pltpu.SemaphoreType.DMA(())])
    return pl.pallas_call(
        functools.partial(_fox_sample_kernel, layer=layer, n_pages=NP),
        grid_spec=grid_spec, out_shape=jax.ShapeDtypeStruct((DB, 8, 64), F32),
        compiler_params=_cparams(1), name="fox_sample",
    )(page_table, cfT, clfT, qb, knb, vnb, lfnb)


def _split2(x):
    a = x.astype(BF16)
    return a, (x - a.astype(F32)).astype(BF16)


def _nsa_sample_kernel(pt_ref, cn_ref, qb_ref, qrow_ref, newb_ref, newrow_ref, vnr_ref, swin_ref,
                       gate_ref, avg8_ref, wbk_ref, wbv_ref, pem_ref, exp_ref,
                       o_ref, nwin_ref,
                       ring, s_ref, m_ref, m8_ref, acc_ref, sn_ref, sem, *, layer, n_pages):
    b = pl.program_id(0)
    NP = n_pages
    P = NP * PAGE_SIZE
    nbp = P // CMP_BLOCK
    NBP = m_ref.shape[0]
    H = NSA_HEADS
    wb = swin_ref.shape[-1]
    kch = exp_ref.shape[1]

    def copy1(i, slot):
        return pltpu.make_async_copy(cn_ref.at[layer, pt_ref[b, i], pl.ds(0, 3)], ring.at[slot], sem.at[slot])

    def copy2(i, slot):
        return pltpu.make_async_copy(cn_ref.at[layer, pt_ref[b, i - NP], 3], ring.at[slot, 0], sem.at[slot])

    def prefetch(nxt):
        @pl.when(nxt < NP)
        def _():
            copy1(nxt, nxt % _NBUF).start()

        @pl.when((nxt >= NP) & (nxt < 2 * NP))
        def _():
            copy2(nxt, nxt % _NBUF).start()

    for s in range(_NBUF - 1):
        prefetch(jnp.int32(s))
    m_ref[...] = jnp.zeros_like(m_ref)

    def body1(i, c):
        slot = i % _NBUF
        copy1(i, slot).wait()
        prefetch(i + _NBUF - 1)
        x1, x2 = _split2(ring[slot, 0:2].reshape(256, 128))
        a = avg8_ref[i % 4]
        contrib = _nt(a, x1) + _nt(a, x2)

        @pl.when(i % 4 == 0)
        def _():
            m8_ref[...] = contrib

        @pl.when(i % 4 != 0)
        def _():
            m8_ref[...] = m8_ref[...] + contrib

        @pl.when(i % 4 == 3)
        def _():
            m_ref[pl.ds(pl.multiple_of((i // 4) * 8, 8), 8), :] = m8_ref[...]

        off = pl.multiple_of(i * 128, 128)
        for h in range(H):
            s_ref[h:h + 1, pl.ds(off, 128)] = jnp.sum(ring[slot, 2, h // NSA_GROUP] * qb_ref[0, h],
                                                      axis=0, keepdims=True)
        return c

    lax.fori_loop(0, NP, body1, 0)

    rowm = lax.broadcasted_iota(jnp.int32, (NBP, 1), 0)
    pem = pem_ref[...]
    mm = m_ref[...] + jnp.where(rowm < nbp, pem, 0.0)
    mm = jnp.where(rowm == nbp, newrow_ref[0, 0:1, :] * (1.0 / CMP_BLOCK) + pem, mm)
    mb = mm.astype(BF16)
    n8 = lax.broadcasted_iota(jnp.int32, (8, NBP), 1)
    n1 = n8[0:1]
    row8 = lax.broadcasted_iota(jnp.int32, (8, NBP), 0)
    rowo = lax.broadcasted_iota(jnp.int32, (8, 128), 0)
    cur = nbp
    o_cs, sels = [], []
    for kv in range(NSA_KV_HEADS):
        kcb = _dot(mb, wbk_ref[kv]).astype(BF16)
        vcb = _dot(mb, wbv_ref[kv]).astype(BF16)
        sc = _nt(qrow_ref[0, kv], kcb)
        valid = (n8 + 1) * CMP_BLOCK - 1 <= P
        s = jnp.where(valid, sc, NEG_INF)
        e = jnp.exp(s - jnp.max(s, axis=1, keepdims=True))
        p_c = jnp.where(valid, e / jnp.sum(e, axis=1, keepdims=True), 0.0)
        o_cs.append(_dot(p_c.astype(BF16), vcb))
        imp = p_c[0:1] + p_c[1:2] + p_c[2:3] + p_c[3:4]
        imp = jnp.where((n1 == cur) | (n1 == 0), FORCED, imp)
        v = jnp.where(n1 > cur, -2.0, imp)
        sel = jnp.zeros((1, NBP), F32)
        for _ in range(N_SELECT):
            mx = jnp.max(v, axis=1, keepdims=True)
            idx = jnp.min(jnp.where(v == mx, n1, 1 << 30), axis=1, keepdims=True)
            hit = n1 == idx
            sel = jnp.where(hit, 1.0, sel)
            v = jnp.where(hit, -3.0, v)
        sels.append(jnp.where(n1 <= cur, sel, 0.0))
    sel8 = jnp.where(row8 < NSA_GROUP, sels[0], sels[1]).astype(BF16)

    mx = jnp.full((H, 1), NEG_INF, F32)
    for ci in range(P // kch):
        mk = _dot(sel8[:, ci * 128:(ci + 1) * 128], exp_ref[...])
        sm = jnp.where(mk > 0.5, s_ref[:, ci * kch:(ci + 1) * kch], NEG_INF)
        s_ref[:, ci * kch:(ci + 1) * kch] = sm
        mx = jnp.maximum(mx, jnp.max(sm, axis=1, keepdims=True))
    for h in range(H):
        kv = h // NSA_GROUP
        sn_ref[h:h + 1, :] = jnp.sum(newb_ref[0, 256 + kv * 64:256 + (kv + 1) * 64, :] * qb_ref[0, h],
                                     axis=0, keepdims=True)
    s_new = sn_ref[...]
    m = jnp.maximum(mx, s_new)

    def pbody(k, lsum):
        off = pl.multiple_of(k * 128, 128)
        p = jnp.exp(s_ref[:, pl.ds(off, 128)] - m)
        s_ref[:, pl.ds(off, 128)] = p
        return lsum + p

    lsum = lax.fori_loop(0, NP, pbody, jnp.zeros((H, 128), F32))
    p_new = jnp.exp(s_new - m)
    denom = jnp.sum(lsum, axis=1, keepdims=True) + p_new[:, 0:1]

    acc_ref[...] = jnp.zeros_like(acc_ref)

    def body2(i, c):
        slot = i % _NBUF
        copy2(i, slot).wait()
        prefetch(i + _NBUF - 1)
        off = pl.multiple_of((i - NP) * 128, 128)
        for h in range(H):
            acc_ref[h] = acc_ref[h] + ring[slot, 0, h // NSA_GROUP] * s_ref[h:h + 1, pl.ds(off, 128)]
        return c

    lax.fori_loop(NP, 2 * NP, body2, 0)

    zpad = jnp.zeros((HEAD_DIM, 128), F32)
    lane5 = lax.broadcasted_iota(jnp.int32, (8, wb), 1)
    lanew = lax.broadcasted_iota(jnp.int32, (HEAD_DIM, wb), 1)
    zkw = jnp.zeros((HEAD_DIM, wb), BF16)
    for kv in range(NSA_KV_HEADS):
        o_s = jnp.zeros((8, 128), F32)
        for g in range(NSA_GROUP):
            h = kv * NSA_GROUP + g
            r = _lane_sum_rows(jnp.concatenate([acc_ref[h], zpad], axis=0))
            r = (r + p_new[h:h + 1, 0:1] * vnr_ref[0, kv, 0]) / denom[h:h + 1]
            o_s = jnp.where(rowo == g, r, o_s)
        qr = qrow_ref[0, kv]
        kw = swin_ref[0, 0, 0, kv]
        vw = swin_ref[0, 0, 1, kv]
        sw = _dot(qr, jnp.concatenate([kw.astype(BF16), zkw], axis=0))
        valid_w = (wb - lane5) < WINDOW
        sw = jnp.where(valid_w, sw, NEG_INF)
        for g in range(NSA_GROUP):
            h = kv * NSA_GROUP + g
            sn_ref[g:g + 1, :] = jnp.sum(newb_ref[0, 512 + kv * 64:512 + (kv + 1) * 64, :] * qb_ref[0, h],
                                         axis=0, keepdims=True)
        s_wn = sn_ref[:, 0:1]
        m_w = jnp.maximum(jnp.max(sw, axis=1, keepdims=True), s_wn)
        p_w = jnp.where(valid_w, jnp.exp(sw - m_w), 0.0)
        p_wn = jnp.exp(s_wn - m_w)
        l_w = jnp.sum(p_w, axis=1, keepdims=True) + p_wn
        o_w = (_nt(p_w.astype(BF16), jnp.concatenate([vw.astype(BF16), zkw], axis=0))
               + p_wn * vnr_ref[0, kv, 1]) / l_w
        o_ref[0, kv] = (gate_ref[0, kv, 0] * o_cs[kv] + gate_ref[0, kv, 1] * o_s
                        + gate_ref[0, kv, 2] * o_w)
        for c in range(2):
            col = newb_ref[0, 512 + c * 128 + kv * 64:512 + c * 128 + (kv + 1) * 64, 0:1]
            rolled = pltpu.roll(swin_ref[0, 0, c, kv], wb - 1, 1)
            nwin_ref[0, c, kv] = jnp.where(lanew == wb - 1, col, rolled)


def _nsa_sample(layer, page_table, cnT, qb, qrow, newb, newrow, vnr, swT, gates, CS):
    DB, NP = page_table.shape
    P = NP * PAGE_SIZE
    wb = swT.shape[-1]
    NBP = -(-(P // CMP_BLOCK + 1) // 128) * 128
    assert NP % 4 == 0
    per_seq = lambda *shape: pl.BlockSpec((1,) + shape, lambda b, pt: (b,) + (0,) * len(shape))
    const = lambda a: pl.BlockSpec(a.shape, lambda b, pt: (0,) * a.ndim)
    consts = [CS["avg8"], CS["wbk"], CS["wbv"], CS["pem"], CS["expand"]]
    grid_spec = pltpu.PrefetchScalarGridSpec(
        num_scalar_prefetch=1, grid=(DB,),
        in_specs=[pl.BlockSpec(memory_space=pl.ANY),
                  per_seq(8, 64, 128), per_seq(2, 8, 128), per_seq(768, 128), per_seq(8, 256),
                  per_seq(2, 2, 8, 128),
                  pl.BlockSpec((1, 1, 2, 2, 64, wb), lambda b, pt: (layer, b, 0, 0, 0, 0)),
                  per_seq(2, 3, 8, 128)] + [const(a) for a in consts],
        out_specs=[per_seq(2, 8, 128), per_seq(2, 2, 64, wb)],
        scratch_shapes=[pltpu.VMEM((_NBUF, 3, 2, 64, 128), F32), pltpu.VMEM((8, P), F32),
                        pltpu.VMEM((NBP, 256), F32), pltpu.VMEM((8, 256), F32),
                        pltpu.VMEM((8, 64, 128), F32), pltpu.VMEM((8, 128), F32),
                        pltpu.SemaphoreType.DMA((_NBUF,))])
    return pl.pallas_call(
        functools.partial(_nsa_sample_kernel, layer=layer, n_pages=NP),
        grid_spec=grid_spec,
        out_shape=[jax.ShapeDtypeStruct((DB, 2, 8, 128), F32),
                   jax.ShapeDtypeStruct((DB, 2, 2, 64, wb), F32)],
        compiler_params=_cparams(1), name="nsa_sample",
    )(page_table, cnT, qb, qrow, newb, newrow, vnr, swT, gates, *consts)


_RING = 16


def _fox_decode_kernel(pt_ref, cf_ref, clf_ref, qb_ref, kn_ref, vn_ref, lfn_ref, o_ref,
                       ring, s_ref, lf_ref, w_ref, t_ref, acc_ref, sn_ref, sem, lsem, *, layer, n_pages):
    b = pl.program_id(0)
    n_seq = pl.num_programs(0)
    NP = n_pages
    H = FOX_HEADS

    def page_copy(seq, i, slot):
        c = i // NP
        pg = i - c * NP
        return pltpu.make_async_copy(cf_ref.at[layer, pt_ref[seq, pg], c], ring.at[slot], sem.at[slot])

    def logf_copy(pg):
        return pltpu.make_async_copy(clf_ref.at[layer, pt_ref[b, pg]], lf_ref.at[pg], lsem)

    def start_logf(pg, c):
        logf_copy(pg).start()
        return c

    lax.fori_loop(0, NP, start_logf, 0)

    @pl.when(b == 0)
    def _():
        for s in range(_RING - 2):
            page_copy(b, s, s).start()

    def advance_pair(i):
        slots = [(i + u) % _RING for u in range(2)]
        for u in range(2):
            page_copy(b, i + u, slots[u]).wait()
        for u in range(2):
            nxt = i + u + _RING - 2

            @pl.when(nxt < 2 * NP)
            def _():
                page_copy(b, nxt, nxt % _RING).start()

            @pl.when((nxt >= 2 * NP) & (b + 1 < n_seq))
            def _():
                page_copy(b + 1, nxt - 2 * NP, nxt % _RING).start()

        return slots

    def kbody(it, c):
        i = it * 2
        slots = advance_pair(i)
        for h in range(H):
            qh = qb_ref[0, h]
            for u in range(2):
                s_ref[i + u, h:h + 1, :] = jnp.sum(ring[slots[u], h] * qh, axis=0, keepdims=True)
        return c

    lax.fori_loop(0, NP // 2, kbody, 0)

    def wait_logf(pg, c):
        logf_copy(pg).wait()
        return c

    lax.fori_loop(0, NP, wait_logf, 0)

    ii = lax.broadcasted_iota(jnp.int32, (128, 128), 0)
    jj = lax.broadcasted_iota(jnp.int32, (128, 128), 1)
    later = jnp.where(ii > jj, 1.0, 0.0).astype(BF16)
    ones = jnp.ones((128, 128), BF16)
    l1, l2, l3 = _split3(lf_ref[...].reshape(NP * 8, 128))
    w_ref[...] = (_dot(l1, later) + _dot(l2, later) + _dot(l3, later)).reshape(NP, 8, 128)
    t_ref[...] = (_dot(l1, ones) + _dot(l2, ones) + _dot(l3, ones)).reshape(NP, 8, 128)
    for h in range(H):
        sn_ref[h:h + 1, :] = jnp.sum(kn_ref[0, h] * qb_ref[0, h], axis=0, keepdims=True)
    cn = lfn_ref[0]
    s_new = sn_ref[...] + cn - cn

    def sbody(k, carry):
        tot, mx = carry
        pg = NP - 1 - k
        sc = s_ref[pg] + cn + w_ref[pg] + tot
        s_ref[pg] = sc
        return tot + t_ref[pg], jnp.maximum(mx, sc)

    _, mx = lax.fori_loop(0, NP, sbody, (jnp.zeros((H, 128), F32), jnp.full((H, 128), NEG_INF, F32)))
    m = jnp.maximum(jnp.max(mx, axis=1, keepdims=True), s_new)
    p = jnp.exp(s_ref[...] - m[None])
    s_ref[...] = p
    p_new = jnp.exp(s_new - m)
    denom = jnp.sum(jnp.sum(p, axis=0), axis=1, keepdims=True) + p_new[:, 0:1]

    acc_ref[...] = jnp.zeros_like(acc_ref)

    def vbody(it, c):
        i = NP + it * 2
        slots = advance_pair(i)
        for h in range(H):
            a = acc_ref[h]
            for u in range(2):
                a = a + ring[slots[u], h] * s_ref[i - NP + u, h:h + 1, :]
            acc_ref[h] = a
        return c

    lax.fori_loop(0, NP // 2, vbody, 0)

    lane = lax.broadcasted_iota(jnp.int32, (HEAD_DIM, 128), 1)
    for h in range(H):
        a = acc_ref[h] + jnp.where(lane == 0, p_new[h:h + 1, :] * vn_ref[0, h], 0.0)
        o_ref[0, h:h + 1, :] = _lane_sum_rows(a)[0:1] / denom[h:h + 1]


def _fox_decode(layer, page_table, cfT, clfT, qb, knb, vnb, lfnb):
    DB, NP = page_table.shape
    assert NP % 2 == 0 and (2 * NP) % _RING == 0
    per_seq = lambda *shape: pl.BlockSpec((1,) + shape, lambda b, pt: (b,) + (0,) * len(shape))
    page_f32 = pltpu.VMEM((NP, 8, 128), F32)
    grid_spec = pltpu.PrefetchScalarGridSpec(
        num_scalar_prefetch=1, grid=(DB,),
        in_specs=[pl.BlockSpec(memory_space=pl.ANY), pl.BlockSpec(memory_space=pl.ANY),
                  per_seq(8, 64, 128), per_seq(8, 64, 128), per_seq(8, 64, 128), per_seq(8, 128)],
        out_specs=per_seq(8, 64),
        scratch_shapes=[pltpu.VMEM((_RING, 8, 64, 128), F32), page_f32, page_f32, page_f32, page_f32,
                        pltpu.VMEM((8, 64, 128), F32), pltpu.VMEM((8, 128), F32),
                        pltpu.SemaphoreType.DMA((_RING,)), pltpu.SemaphoreType.DMA(())])
    return pl.pallas_call(
        functools.partial(_fox_decode_kernel, layer=layer, n_pages=NP),
        grid_spec=grid_spec, out_shape=jax.ShapeDtypeStruct((DB, 8, 64), F32),
        compiler_params=_cparams(1), name="fox_sample",
    )(page_table, cfT, clfT, qb, knb, vnb, lfnb)


def _nsa_decode_kernel(pt_ref, cn_ref, qb_ref, qrow_ref, newb_ref, newrow_ref, vnr_ref, swin_ref,
                       gate_ref, avg8_ref, wbk_ref, wbv_ref, pem_ref, exp_ref,
                       o_ref, nwin_ref,
                       ring, s_ref, m_ref, acc_ref, sn_ref, sem, *, layer, n_pages):
    b = pl.program_id(0)
    NP = n_pages
    G = NP // 4
    P = NP * PAGE_SIZE
    nbp = P // CMP_BLOCK
    NBP = m_ref.shape[0]
    H = NSA_HEADS
    wb = swin_ref.shape[-1]
    kch = exp_ref.shape[1]

    n_seq = pl.num_programs(0)
    n_grp = _RING // 4
    ahead = n_grp

    def copy1(seq, pg, slot):
        return pltpu.make_async_copy(cn_ref.at[layer, pt_ref[seq, pg], pl.ds(0, 3)], ring.at[slot], sem.at[slot])

    def copy2(seq, pg, slot):
        return pltpu.make_async_copy(cn_ref.at[layer, pt_ref[seq, pg], 3], ring.at[slot, 0], sem.at[slot])

    def start_group(seq, gi):
        base = (gi % n_grp) * 4

        @pl.when(gi < G)
        def _():
            for u in range(4):
                copy1(seq, gi * 4 + u, base + u).start()

        @pl.when(gi >= G)
        def _():
            for u in range(4):
                copy2(seq, (gi - G) * 4 + u, base + u).start()

    def start_ahead(g):
        nxt = g + ahead

        @pl.when(nxt < 2 * G)
        def _():
            start_group(b, nxt)

        @pl.when((nxt >= 2 * G) & (b + 1 < n_seq))
        def _():
            start_group(b + 1, nxt - 2 * G)

    @pl.when(b == 0)
    def _():
        for gi in range(ahead):
            start_group(b, jnp.int32(gi))

    m_ref[...] = jnp.zeros_like(m_ref)
    zk64 = jnp.zeros((HEAD_DIM, 128), BF16)

    def body1(g, c):
        base = (g % n_grp) * 4
        for u in range(4):
            copy1(b, g * 4 + u, base + u).wait()
        contrib = jnp.zeros((8, 256), F32)
        for u in range(4):
            x1, x2 = _split2(ring[base + u, 0:2].reshape(256, 128))
            contrib = contrib + _nt(avg8_ref[u], x1) + _nt(avg8_ref[u], x2)
        m_ref[pl.ds(pl.multiple_of(g * 8, 8), 8), :] = contrib
        for u in range(4):
            off = pl.multiple_of(g * 512 + u * 128, 128)
            ks = ring[base + u, 2].astype(BF16)
            s_ref[:, pl.ds(off, 128)] = (_dot(qrow_ref[0, 0], jnp.concatenate([ks[0], zk64], axis=0))
                                         + _dot(qrow_ref[0, 1], jnp.concatenate([ks[1], zk64], axis=0)))
        start_ahead(g)
        return c

    lax.fori_loop(0, G, body1, 0)

    rowm = lax.broadcasted_iota(jnp.int32, (NBP, 1), 0)
    pem = pem_ref[...]
    mm = m_ref[...] + jnp.where(rowm < nbp, pem, 0.0)
    mm = jnp.where(rowm == nbp, newrow_ref[0, 0:1, :] * (1.0 / CMP_BLOCK) + pem, mm)
    mb = mm.astype(BF16)
    n8 = lax.broadcasted_iota(jnp.int32, (8, NBP), 1)
    row8 = lax.broadcasted_iota(jnp.int32, (8, NBP), 0)
    rowo = lax.broadcasted_iota(jnp.int32, (8, 128), 0)
    cur = nbp
    kcb = [_dot(mb, wbk_ref[kv]).astype(BF16) for kv in range(NSA_KV_HEADS)]
    vcb = [_dot(mb, wbv_ref[kv]).astype(BF16) for kv in range(NSA_KV_HEADS)]
    sc = _nt(qrow_ref[0, 0], kcb[0]) + _nt(qrow_ref[0, 1], kcb[1])
    valid = (n8 + 1) * CMP_BLOCK - 1 <= P
    s = jnp.where(valid, sc, NEG_INF)
    e = jnp.exp(s - jnp.max(s, axis=1, keepdims=True))
    p_c = jnp.where(valid, e / jnp.sum(e, axis=1, keepdims=True), 0.0)
    o_c = (_dot(jnp.where(row8 < NSA_GROUP, p_c, 0.0).astype(BF16), vcb[0])
           + _dot(jnp.where(row8 >= NSA_GROUP, p_c, 0.0).astype(BF16), vcb[1]))
    imp0 = p_c[0:1] + p_c[1:2] + p_c[2:3] + p_c[3:4]
    imp1 = p_c[4:5] + p_c[5:6] + p_c[6:7] + p_c[7:8]
    n1 = n8[0:1]
    rowp = lax.broadcasted_iota(jnp.int32, (128, NBP), 0)
    m_idx = lax.broadcasted_iota(jnp.int32, (NBP, NBP), 0)
    n_idx = lax.broadcasted_iota(jnp.int32, (NBP, NBP), 1)
    imps = []
    for imp in (imp0, imp1):
        imp = jnp.where((n1 == cur) | (n1 == 0), FORCED, imp)
        imps.append(jnp.where(n1 > cur, -2.0, imp))
    impT = jnp.where(rowp == 0, imps[0], jnp.where(rowp == 1, imps[1], 0.0)).T
    sels = []
    for kv in range(NSA_KV_HEADS):
        col = impT[:, kv:kv + 1]
        beats = (col > imps[kv]) | ((col == imps[kv]) & (m_idx < n_idx))
        rank = jnp.sum(jnp.where(beats, 1.0, 0.0), axis=0, keepdims=True)
        sels.append(jnp.where((rank < N_SELECT) & (n1 <= cur), 1.0, 0.0))
    sel8 = jnp.where(row8 < NSA_GROUP, sels[0], sels[1]).astype(BF16)

    mx = jnp.full((H, 1), NEG_INF, F32)
    for ci in range(P // kch):
        mk = _dot(sel8[:, ci * 128:(ci + 1) * 128], exp_ref[...])
        sm = jnp.where(mk > 0.5, s_ref[:, ci * kch:(ci + 1) * kch], NEG_INF)
        s_ref[:, ci * kch:(ci + 1) * kch] = sm
        mx = jnp.maximum(mx, jnp.max(sm, axis=1, keepdims=True))
    for h in range(H):
        kv = h // NSA_GROUP
        sn_ref[h:h + 1, :] = jnp.sum(newb_ref[0, 256 + kv * 64:256 + (kv + 1) * 64, :] * qb_ref[0, h],
                                     axis=0, keepdims=True)
    s_new = sn_ref[...]
    m = jnp.maximum(mx, s_new)
    lsum = jnp.zeros((H, 1), F32)
    for ci in range(P // kch):
        p = jnp.exp(s_ref[:, ci * kch:(ci + 1) * kch] - m[:, 0:1])
        s_ref[:, ci * kch:(ci + 1) * kch] = p
        lsum = lsum + jnp.sum(p, axis=1, keepdims=True)
    p_new = jnp.exp(s_new - m)
    denom = lsum + p_new[:, 0:1]

    acc_ref[...] = jnp.zeros_like(acc_ref)

    def body2(g, c):
        base = (g % n_grp) * 4
        for u in range(4):
            copy2(b, (g - G) * 4 + u, base + u).wait()
        for h in range(H):
            a = acc_ref[h]
            for u in range(4):
                off = pl.multiple_of((g - G) * 512 + u * 128, 128)
                a = a + ring[base + u, 0, h // NSA_GROUP] * s_ref[h:h + 1, pl.ds(off, 128)]
            acc_ref[h] = a
        start_ahead(g)
        return c

    lax.fori_loop(G, 2 * G, body2, 0)

    zpad = jnp.zeros((HEAD_DIM, 128), F32)
    o_s = jnp.zeros((8, 128), F32)
    for h in range(H):
        r = _lane_sum_rows(jnp.concatenate([acc_ref[h], zpad], axis=0))
        o_s = jnp.where(rowo == h, r, o_s)
    o_s = (o_s + p_new[:, 0:1] * vnr_ref[0, 0]) / denom

    lane5 = lax.broadcasted_iota(jnp.int32, (8, wb), 1)
    row5 = lax.broadcasted_iota(jnp.int32, (8, wb), 0)
    zkw = jnp.zeros((HEAD_DIM, wb), BF16)
    sw = jnp.zeros((8, wb), F32)
    for kv in range(NSA_KV_HEADS):
        sw = sw + _dot(qrow_ref[0, kv], jnp.concatenate([swin_ref[0, 0, 0, kv].astype(BF16), zkw], axis=0))
    valid_w = (wb - lane5) < WINDOW
    sw = jnp.where(valid_w, sw, NEG_INF)
    for h in range(H):
        kv = h // NSA_GROUP
        sn_ref[h:h + 1, :] = jnp.sum(newb_ref[0, 512 + kv * 64:512 + (kv + 1) * 64, :] * qb_ref[0, h],
                                     axis=0, keepdims=True)
    s_wn = sn_ref[:, 0:1]
    m_w = jnp.maximum(jnp.max(sw, axis=1, keepdims=True), s_wn)
    p_w = jnp.where(valid_w, jnp.exp(sw - m_w), 0.0)
    p_wn = jnp.exp(s_wn - m_w)
    l_w = jnp.sum(p_w, axis=1, keepdims=True) + p_wn
    o_w = p_wn * vnr_ref[0, 1]
    for kv in range(NSA_KV_HEADS):
        mine = (row5 < NSA_GROUP) if kv == 0 else (row5 >= NSA_GROUP)
        o_w = o_w + _nt(jnp.where(mine, p_w, 0.0).astype(BF16),
                        jnp.concatenate([swin_ref[0, 0, 1, kv].astype(BF16), zkw], axis=0))
    o_w = o_w / l_w
    o_ref[0] = gate_ref[0, 0] * o_c + gate_ref[0, 1] * o_s + gate_ref[0, 2] * o_w

    lanew = lax.broadcasted_iota(jnp.int32, (HEAD_DIM, wb), 1)
    for kv in range(NSA_KV_HEADS):
        for c in range(2):
            col = newb_ref[0, 512 + c * 128 + kv * 64:512 + c * 128 + (kv + 1) * 64, 0:1]
            rolled = pltpu.roll(swin_ref[0, 0, c, kv], wb - 1, 1)
            nwin_ref[0, c, kv] = jnp.where(lanew == wb - 1, col, rolled)


def _nsa_decode(layer, page_table, cnT, qb, qrow, newb, newrow, vnr, swT, gates, CS):
    DB, NP = page_table.shape
    P = NP * PAGE_SIZE
    wb = swT.shape[-1]
    NBP = -(-(P // CMP_BLOCK + 1) // 128) * 128
    assert NP % 4 == 0 and (2 * NP) % _RING == 0
    per_seq = lambda *shape: pl.BlockSpec((1,) + shape, lambda b, pt: (b,) + (0,) * len(shape))
    const = lambda a: pl.BlockSpec(a.shape, lambda b, pt: (0,) * a.ndim)
    consts = [CS["avg8"], CS["wbk"], CS["wbv"], CS["pem"], CS["expand"]]
    grid_spec = pltpu.PrefetchScalarGridSpec(
        num_scalar_prefetch=1, grid=(DB,),
        in_specs=[pl.BlockSpec(memory_space=pl.ANY),
                  per_seq(8, 64, 128), per_seq(2, 8, 128), per_seq(768, 128), per_seq(8, 256),
                  per_seq(2, 8, 128),
                  pl.BlockSpec((1, 1, 2, 2, 64, wb), lambda b, pt: (layer, b, 0, 0, 0, 0)),
                  per_seq(3, 8, 128)] + [const(a) for a in consts],
        out_specs=[per_seq(8, 128), per_seq(2, 2, 64, wb)],
        scratch_shapes=[pltpu.VMEM((_RING, 3, 2, 64, 128), F32), pltpu.VMEM((8, P), F32),
                        pltpu.VMEM((NBP, 256), F32), pltpu.VMEM((8, 64, 128), F32),
                        pltpu.VMEM((8, 128), F32), pltpu.SemaphoreType.DMA((_RING,))])
    return pl.pallas_call(
        functools.partial(_nsa_decode_kernel, layer=layer, n_pages=NP),
        grid_spec=grid_spec,
        out_shape=[jax.ShapeDtypeStruct((DB, 8, 128), F32),
                   jax.ShapeDtypeStruct((DB, 2, 2, 64, wb), F32)],
        compiler_params=_cparams(1), name="nsa_sample",
    )(page_table, cnT, qb, qrow, newb, newrow, vnr, swT, gates, *consts)


def _sample_consts(P, L):
    r = jnp.arange(128)
    avg8 = jnp.stack([jnp.where(jnp.arange(8)[:, None] == 2 * p + r[None, :] // CMP_BLOCK,
                                1.0 / CMP_BLOCK, 0.0) for p in range(4)]).astype(BF16)
    w = L["w_cmp"]
    z = jnp.zeros((64, 128), F32)
    wpad = lambda c: jnp.concatenate([w[c], jnp.zeros((64, 64), F32)], axis=1)
    blocks = lambda c, kv: jnp.concatenate(
        [wpad(c) if (cc, kk) == (c, kv) else z for cc in range(2) for kk in range(2)], axis=0)
    wbk = jnp.stack([blocks(0, kv) for kv in range(2)]).astype(BF16)
    wbv = jnp.stack([blocks(1, kv) for kv in range(2)]).astype(BF16)
    pm = jnp.mean(L["pe"], axis=1)
    pem = jnp.concatenate([pm[0], pm[0], pm[1], pm[1]])[None, :]
    kch = min(P, 8192)
    expand = (r[:, None] == (jnp.arange(kch) // CMP_BLOCK)[None, :]).astype(BF16)
    return dict(avg8=avg8, wbk=wbk, wbv=wbv, pem=pem, expand=expand)


def _sample_layer(layer, xs, p_l, L, cos_s, sin_s, page_table, cnT, cfT, clfT, swT):
    DB, D = xs.shape
    P = page_table.shape[1] * PAGE_SIZE
    xpad = jnp.zeros((1, 128, D), F32).at[0, :DB].set(xs)
    nsaT, winT, foxT, logfT, qn, qf, kf, kvb, ga, gm, qT = _proj(
        xpad, L["gmix"], L["wt"], L["wg"], cos_s, sin_s, L["hg"], L["fb"], 128)
    lanes = lambda a: jnp.broadcast_to(a[..., None], a.shape + (128,))
    q_all = (qT[0, :, :DB] * SCALE).T
    qb_a = lanes(q_all[:, :512].reshape(DB, 8, 64))
    qb_f = lanes(q_all[:, 512:].reshape(DB, 8, 64))
    new = jnp.concatenate([nsaT[0], winT[0]], axis=0)[:, :DB].T
    fx = foxT[0][:, :DB].T
    lf = logfT[0][:, :DB].T
    q8 = jnp.transpose(qn[0, :, :DB, :], (1, 0, 2))
    zq = jnp.zeros_like(q8[:, :4])
    qrow = jnp.stack([jnp.concatenate([q8[:, :4], zq], axis=1),
                      jnp.concatenate([zq, q8[:, 4:]], axis=1)], axis=1)
    g = jnp.transpose(ga[0, :, :DB, :12].reshape(2, DB, 4, 3), (1, 3, 0, 2))
    gates = lanes(g.reshape(DB, 3, 8))
    vrow = jnp.stack([new[:, 384:512].reshape(DB, 2, 64), new[:, 640:768].reshape(DB, 2, 64)], axis=1)
    vrow = jnp.repeat(vrow, NSA_GROUP, axis=2)
    vnr = jnp.concatenate([vrow, jnp.zeros_like(vrow)], axis=-1)
    newrow = jnp.broadcast_to(new[:, None, :256], (DB, 8, 256))
    CS = _sample_consts(P, L)
    o_a8, new_win = _nsa_decode(layer, page_table, cnT, qb_a, qrow, lanes(new), newrow, vnr, swT, gates, CS)
    o_a = o_a8[:, :, :64].reshape(DB, 512)
    o_b = _fox_decode(layer, page_table, cfT, clfT, qb_f, lanes(fx[:, :512].reshape(DB, 8, 64)),
                      lanes(fx[:, 512:].reshape(DB, 8, 64)), lanes(lf)).reshape(DB, 512)
    y = _tail(xs, o_a, o_b, gm[0, :DB], p_l, L["tail"], DB)
    return (y, new[:, :512].reshape(DB, 1, 4, NSA_KV_HEADS, HEAD_DIM),
            fx.reshape(DB, 1, 2, FOX_HEADS, HEAD_DIM), lf.reshape(DB, 1, FOX_HEADS),
            jnp.transpose(new_win, (0, 4, 1, 2, 3)))


def kernel(x_prompt, x_sample, p_prompt, p_sample, cache_nsa, cache_fox, cache_fox_logf, state_win, page_table, norm_mix, w_in, q_norm_nsa, k_norm_nsa, q_norm_fox, k_norm_fox, fox_fbias, cmp_pe, w_cmp, w_up_nsa, w_up_fox, w_out, norm_ffn, w_ffn_gate, w_ffn_up, w_ffn_down, norm_ple, w_ple_gate, w_ple_proj):
    depth = w_in.shape[0]
    B, T, D = x_prompt.shape
    DB, S, _ = x_sample.shape
    assert S == 1 and DB <= 128
    P = page_table.shape[1] * PAGE_SIZE
    cnT = jnp.transpose(cache_nsa, (0, 1, 3, 4, 5, 2))
    cfT = jnp.transpose(cache_fox, (0, 1, 3, 4, 5, 2))
    clfT = jnp.transpose(cache_fox_logf, (0, 1, 3, 2))
    swT = jnp.transpose(state_win, (0, 1, 3, 4, 5, 2))
    C = _prompt_consts(T)
    cos_s, sin_s = _rope_tables(jnp.full((128,), P, jnp.int32))
    tm, tq = min(PROJ_TM, T), min(FOX_TQ, T)
    xp, xs = x_prompt, x_sample.reshape(DB, D)
    outs = [[] for _ in range(8)]
    for l in range(depth):
        L = _prep_layer(l, norm_mix, w_in, q_norm_nsa, k_norm_nsa, q_norm_fox, k_norm_fox, fox_fbias,
                        cmp_pe, w_cmp, w_up_nsa, w_up_fox, w_out, norm_ffn, w_ffn_gate, w_ffn_up,
                        w_ffn_down, norm_ple, w_ple_gate, w_ple_proj)
        xp, nsa_p, fox_p, lf_p, win_p = _prompt_layer(xp, p_prompt[l], L, C, tm, tq, min(TAIL_TM, T))
        xs, nsa_s, fox_s, lf_s, win_s = _sample_layer(l, xs, p_sample[l].reshape(DB, -1), L, cos_s, sin_s,
                                                      page_table, cnT, cfT, clfT, swT)
        for acc, v in zip(outs, (nsa_p, nsa_s, fox_p, fox_s, lf_p, lf_s, win_p, win_s)):
            acc.append(v)
    return (xp, xs.reshape(DB, 1, D)) + tuple(jnp.stack(o) for o in outs)
```

```python
import functools

import jax
import jax.numpy as jnp
from jax import lax
from jax.experimental import pallas as pl
from jax.experimental.pallas import tpu as pltpu

HEAD_DIM = 64
NSA_HEADS = 8
NSA_KV_HEADS = 2
NSA_GROUP = NSA_HEADS // NSA_KV_HEADS
FOX_HEADS = 8
CMP_BLOCK = 64
N_SELECT = 16
WINDOW = 512
PAGE_SIZE = 128
ROPE_THETA = 10000.0
EPS = 1e-6
NEG_INF = -1e30
FORCED = 1e4
SCALE = HEAD_DIM ** -0.5

PROJ_TM = 256
TAIL_TM = 256
FOX_TQ, FOX_TK = 512, 1024
NSA_CH = 1024
VMEM_LIMIT = 56 * 1024 * 1024
UNSELECTED = NEG_INF

F32 = jnp.float32
BF16 = jnp.bfloat16

_R_QA, _R_KV6, _R_QB, _R_KB, _R_VB, _R_FB, _R_END = 0, 512, 1280, 1792, 2304, 2816, 2832


def _nt(a, b):
    return lax.dot_general(a, b, (((1,), (1,)), ((), ())), preferred_element_type=F32)


def _dot(a, b):
    return jnp.dot(a, b, preferred_element_type=F32)


def _split3(x):
    a = x.astype(BF16)
    r = x - a.astype(F32)
    b = r.astype(BF16)
    c = (r - b.astype(F32)).astype(BF16)
    return a, b, c


def _log_sigmoid(x):
    return jnp.minimum(x, 0.0) - jnp.log1p(jnp.exp(-jnp.abs(x)))


def _cparams(n_grid):
    return pltpu.CompilerParams(dimension_semantics=("arbitrary",) * n_grid,
                                vmem_limit_bytes=VMEM_LIMIT)


def _proj_kernel(x_ref, gmix_ref, wt_ref, wg_ref, cos_ref, sin_ref, hg_ref, fb_ref,
                 nsaT_ref, winT_ref, foxT_ref, logfT_ref, qn_ref, qf_ref, kf_ref, kvb_ref,
                 ga_ref, gm_ref, qT_ref, carry_ref, *, tm):
    j = pl.program_id(1)

    @pl.when(j == 0)
    def _():
        carry_ref[...] = jnp.zeros_like(carry_ref)

    x = x_ref[0]
    ms = jnp.mean(x * x, axis=-1, keepdims=True)
    h = (x * lax.rsqrt(ms + EPS) * gmix_ref[...]).astype(BF16)
    cos = cos_ref[...]
    sin = sin_ref[...]
    hg = hg_ref[...]

    def zt(r0, r1):
        return _nt(wt_ref[r0:r1, :], h)

    def headnorm(z, col):
        ss = jnp.sum(z * z, axis=0, keepdims=True)
        return z * lax.rsqrt(ss * (1.0 / HEAD_DIM) + EPS) * hg[:, col:col + 1]

    def rope(z):
        x1, x2 = z[:HEAD_DIM // 2], z[HEAD_DIM // 2:]
        return jnp.concatenate([x1 * cos - x2 * sin, x2 * cos + x1 * sin], axis=0)

    zeros64 = jnp.zeros((HEAD_DIM, tm), F32)
    row8 = lax.broadcasted_iota(jnp.int32, (8, tm), 0)

    for hp in range(NSA_HEADS // 2):
        z = zt(_R_QA + hp * 128, _R_QA + (hp + 1) * 128)
        for i in range(2):
            hd = hp * 2 + i
            q = rope(headnorm(z[i * 64:(i + 1) * 64], 0))
            qT_ref[0, hd * 64:(hd + 1) * 64, :] = q
            blk = jnp.concatenate([q * SCALE, zeros64], axis=0)
            qn_ref[0, hd] = blk.T.astype(BF16)

    for part in range(6):
        z = zt(_R_KV6 + part * 128, _R_KV6 + (part + 1) * 128)
        if part % 2 == 0:
            z = jnp.concatenate([rope(headnorm(z[:64], 1 + part // 2)),
                                 rope(headnorm(z[64:], 1 + part // 2))], axis=0)
        if part < 4:
            nsaT_ref[0, part * 128:(part + 1) * 128, :] = z
        else:
            winT_ref[0, (part - 4) * 128:(part - 3) * 128, :] = z
        if part >= 2:
            kvb_ref[0, (part - 2) * 128:(part - 1) * 128, :] = z.astype(BF16)

    zf = zt(_R_FB, _R_END)[:8]
    logf = _log_sigmoid(zf + fb_ref[...])
    logfT_ref[0] = logf
    ii = lax.broadcasted_iota(jnp.int32, (tm, tm), 0)
    jj = lax.broadcasted_iota(jnp.int32, (tm, tm), 1)
    tri = jnp.where(ii <= jj, 1.0, 0.0).astype(BF16)
    l1, l2, l3 = _split3(logf)
    c = carry_ref[:, 0:1] + (_dot(l1, tri) + _dot(l2, tri) + _dot(l3, tri))
    carry_ref[...] = jnp.broadcast_to(c[:, tm - 1:tm], carry_ref.shape)
    c1, c2, c3 = [p.astype(F32) for p in _split3(c)]

    for hp in range(FOX_HEADS // 2):
        z = zt(_R_QB + hp * 128, _R_QB + (hp + 1) * 128)
        for i in range(2):
            hd = hp * 2 + i
            q = headnorm(z[i * 64:(i + 1) * 64], 4)
            qT_ref[0, 512 + hd * 64:512 + (hd + 1) * 64, :] = q
            aug = jnp.where(row8 == 0, c1[hd:hd + 1],
                            jnp.where(row8 == 1, c2[hd:hd + 1],
                                      jnp.where(row8 == 2, c3[hd:hd + 1],
                                                jnp.where(row8 < 6, 1.0, 0.0))))
            blk = jnp.concatenate([q * SCALE, aug, jnp.zeros((56, tm), F32)], axis=0)
            qf_ref[0, hd] = blk.T.astype(BF16)

    for hp in range(FOX_HEADS // 2):
        z = zt(_R_KB + hp * 128, _R_KB + (hp + 1) * 128)
        for i in range(2):
            hd = hp * 2 + i
            k = headnorm(z[i * 64:(i + 1) * 64], 5)
            foxT_ref[0, hd * 64:(hd + 1) * 64, :] = k
            aug = jnp.where(row8 < 3, 1.0,
                            jnp.where(row8 == 3, -c1[hd:hd + 1],
                                      jnp.where(row8 == 4, -c2[hd:hd + 1],
                                                jnp.where(row8 == 5, -c3[hd:hd + 1], 0.0))))
            blk = jnp.concatenate([k, aug, jnp.zeros((56, tm), F32)], axis=0)
            kf_ref[0, hd] = blk.astype(BF16)

    for hp in range(FOX_HEADS // 2):
        z = zt(_R_VB + hp * 128, _R_VB + (hp + 1) * 128)
        foxT_ref[0, 512 + hp * 128:512 + (hp + 1) * 128, :] = z
        kvb_ref[0, 512 + hp * 128:512 + (hp + 1) * 128, :] = z.astype(BF16)

    zg = _dot(h, wg_ref[:, 0:256])
    ga_ref[0, 0] = jax.nn.sigmoid(zg[:, 0:128])
    ga_ref[0, 1] = jax.nn.sigmoid(zg[:, 128:256])
    ngm = gm_ref.shape[-1]
    for c0 in range(0, ngm, 512):
        gm_ref[0, :, c0:c0 + 512] = jax.nn.sigmoid(_dot(h, wg_ref[:, 256 + c0:256 + c0 + 512]))


def _proj(x, gmix, wt, wg, cos, sin, hg, fb, tm):
    B, T, D = x.shape
    grid = (B, T // tm)
    ngm = wg.shape[1] - 256
    const = lambda *shape: pl.BlockSpec(shape, lambda b, j: (0,) * len(shape), pipeline_mode=pl.Buffered(1))
    out_shape = [
        jax.ShapeDtypeStruct((B, 512, T), F32),
        jax.ShapeDtypeStruct((B, 256, T), F32),
        jax.ShapeDtypeStruct((B, 1024, T), F32),
        jax.ShapeDtypeStruct((B, 8, T), F32),
        jax.ShapeDtypeStruct((B, NSA_HEADS, T, 128), BF16),
        jax.ShapeDtypeStruct((B, FOX_HEADS, T, 128), BF16),
        jax.ShapeDtypeStruct((B, FOX_HEADS, 128, T), BF16),
        jax.ShapeDtypeStruct((B, 1024, T), BF16),
        jax.ShapeDtypeStruct((B, 2, T, 128), F32),
        jax.ShapeDtypeStruct((B, T, ngm), F32),
        jax.ShapeDtypeStruct((B, 1024, T), F32),
    ]
    out_specs = [
        pl.BlockSpec((1, 512, tm), lambda b, j: (b, 0, j)),
        pl.BlockSpec((1, 256, tm), lambda b, j: (b, 0, j)),
        pl.BlockSpec((1, 1024, tm), lambda b, j: (b, 0, j)),
        pl.BlockSpec((1, 8, tm), lambda b, j: (b, 0, j)),
        pl.BlockSpec((1, NSA_HEADS, tm, 128), lambda b, j: (b, 0, j, 0)),
        pl.BlockSpec((1, FOX_HEADS, tm, 128), lambda b, j: (b, 0, j, 0)),
        pl.BlockSpec((1, FOX_HEADS, 128, tm), lambda b, j: (b, 0, 0, j)),
        pl.BlockSpec((1, 1024, tm), lambda b, j: (b, 0, j)),
        pl.BlockSpec((1, 2, tm, 128), lambda b, j: (b, 0, j, 0)),
        pl.BlockSpec((1, tm, ngm), lambda b, j: (b, j, 0)),
        pl.BlockSpec((1, 1024, tm), lambda b, j: (b, 0, j)),
    ]
    in_specs = [
        pl.BlockSpec((1, tm, D), lambda b, j: (b, j, 0)),
        const(1, D),
        const(*wt.shape),
        const(*wg.shape),
        pl.BlockSpec((32, tm), lambda b, j: (0, j)),
        pl.BlockSpec((32, tm), lambda b, j: (0, j)),
        const(64, 8),
        const(8, 1),
    ]
    return pl.pallas_call(
        functools.partial(_proj_kernel, tm=tm),
        grid=grid, in_specs=in_specs, out_specs=out_specs, out_shape=out_shape,
        scratch_shapes=[pltpu.VMEM((8, 128), F32)],
        compiler_params=_cparams(2), name="proj",
    )(x, gmix, wt, wg, cos, sin, hg, fb)


def _fox_kernel(q_ref, k_ref, v_ref, o_ref, *, tq, tk):
    j = pl.program_id(2)
    row = lax.broadcasted_iota(jnp.int32, (128, tk), 0)
    lane_o = lax.broadcasted_iota(jnp.int32, (tq, 128), 1)
    one_row = (64, 0)
    keep = [jnp.where((row < 64) if i == 0 else (row >= 64), 1.0, 0.0).astype(BF16) for i in range(2)]
    fill = [jnp.where(row == one_row[i], 1.0, 0.0).astype(BF16) for i in range(2)]
    qs = [q_ref[0, i] for i in range(2)]

    def step(c, carry, causal):
        off = pl.multiple_of(c * tk, tk)
        vs = v_ref[0, :, pl.ds(off, tk)]
        out = []
        for i in range(2):
            m, acc = carry[i]
            s = _dot(qs[i], k_ref[0, i, :, pl.ds(off, tk)])
            if causal:
                rr = j * tq + lax.broadcasted_iota(jnp.int32, (tq, tk), 0)
                cc = c * tk + lax.broadcasted_iota(jnp.int32, (tq, tk), 1)
                s = jnp.where(cc <= rr, s, NEG_INF)
            m_new = jnp.maximum(m, jnp.max(s, axis=1, keepdims=True))
            p = jnp.exp(s - m_new).astype(BF16)
            acc = jnp.exp(m - m_new) * acc + _nt(p, vs * keep[i] + fill[i])
            out.append((m_new, acc))
        return tuple(out)

    init = tuple((jnp.full((tq, 1), NEG_INF, F32), jnp.zeros((tq, 128), F32)) for _ in range(2))
    last = (j * tq) // tk
    carry = lax.fori_loop(0, last, functools.partial(step, causal=False), init)
    (_, a0), (_, a1) = step(last, carry, True)
    o_ref[0] = jnp.where(lane_o < 64, a0 / a0[:, 64:65], a1 / a1[:, 0:1])


def _fox_prompt(qf, kf, kvb, tq):
    B, H, T, _ = qf.shape
    grid = (B, H // 2, T // tq)
    tk = min(FOX_TK, T)
    assert T % tk == 0 and tk % tq == 0
    return pl.pallas_call(
        functools.partial(_fox_kernel, tq=tq, tk=tk),
        grid=grid,
        in_specs=[
            pl.BlockSpec((1, 2, tq, 128), lambda b, hp, j: (b, hp, j, 0)),
            pl.BlockSpec((1, 2, 128, T), lambda b, hp, j: (b, hp, 0, 0)),
            pl.BlockSpec((1, 128, T), lambda b, hp, j: (b, 4 + hp, 0)),
        ],
        out_specs=pl.BlockSpec((1, tq, 128), lambda b, hp, j: (b, j, hp)),
        out_shape=jax.ShapeDtypeStruct((B, T, 512), F32),
        compiler_params=_cparams(3), name="fox_prompt",
    )(qf, kf, kvb)


def _cmp_kernel(xk_ref, xv_ref, avg_ref, avgT_ref, peT_ref, pe_ref, w0T_ref, w1_ref,
                kcb_ref, vcb_ref):
    lane = lax.broadcasted_iota(jnp.int32, (128, 128), 1)
    rowi = lax.broadcasted_iota(jnp.int32, (128, 128), 0)
    xk = xk_ref[0]
    k1, k2, _ = _split3(xk)
    mk = _dot(k1, avg_ref[...]) + _dot(k2, avg_ref[...])
    mk = mk + jnp.mean(peT_ref[0], axis=1, keepdims=True)
    kcb = _dot(w0T_ref[...], mk.astype(BF16))
    kcb_ref[0, 0] = jnp.where(lane >= 64, kcb, 0.0).astype(BF16)
    xv = xv_ref[0]
    v1, v2, _ = _split3(xv)
    mv = _nt(avgT_ref[...], v1) + _nt(avgT_ref[...], v2)
    mv = mv + jnp.mean(pe_ref[1], axis=0, keepdims=True)
    vcb = _dot(mv.astype(BF16), w1_ref[...])
    vcb_ref[0, 0] = jnp.where(rowi >= 64, vcb, 0.0).astype(BF16)


def _cmp_prompt(nsaT, avg, avgT, peT, pe, w0T, w1):
    B, _, T = nsaT.shape
    const = lambda *shape: pl.BlockSpec(shape, lambda b, kv: (0,) * len(shape))
    return pl.pallas_call(
        _cmp_kernel,
        grid=(B, NSA_KV_HEADS),
        in_specs=[
            pl.BlockSpec((1, 64, T), lambda b, kv: (b, kv, 0)),
            pl.BlockSpec((1, 64, T), lambda b, kv: (b, 2 + kv, 0)),
            const(T, 128), const(128, T), const(2, 64, 64), const(2, 64, 64),
            const(128, 64), const(64, 128),
        ],
        out_specs=[pl.BlockSpec((1, 1, 128, 128), lambda b, kv: (b, kv, 0, 0))] * 2,
        out_shape=[jax.ShapeDtypeStruct((B, NSA_KV_HEADS, 128, 128), BF16)] * 2,
        compiler_params=_cparams(2), name="cmp_prompt",
    )(nsaT, nsaT, avg, avgT, peT, pe, w0T, w1)


def _nsa_kernel(q_ref, kcb_ref, vcb_ref, ks_ref, vs_ref, kw_ref, vw_ref, oh_ref, ga_ref,
                o_ref, *, nb, ch, ww, n_tiles):
    j = pl.program_id(2)
    QB = 128
    R = NSA_GROUP * QB
    q4 = q_ref[0].reshape(R, 128)
    lane = lax.broadcasted_iota(jnp.int32, (QB, 128), 1)
    n_idx = lane - 64
    r_q = lax.broadcasted_iota(jnp.int32, (QB, 128), 0)
    t = j * QB + r_q
    real = (lane >= 64) & (n_idx < nb)
    rep4 = lambda a: jnp.concatenate([a] * NSA_GROUP, axis=0)
    lane4 = lax.broadcasted_iota(jnp.int32, (R, 128), 1)
    t4 = j * QB + lax.broadcasted_iota(jnp.int32, (R, 128), 0) % QB

    s_c = _dot(q4, kcb_ref[0, 0])
    valid_c = (lane4 >= 64) & (lane4 - 64 < nb) & ((lane4 - 63) * CMP_BLOCK - 1 <= t4)
    s = jnp.where(valid_c, s_c, NEG_INF)
    e = jnp.exp(s - jnp.max(s, axis=1, keepdims=True))
    p_c = jnp.where(valid_c, e / jnp.sum(e, axis=1, keepdims=True), 0.0)
    o_c = _dot(p_c.astype(BF16), vcb_ref[0, 0])

    imp = p_c[0:QB] + p_c[QB:2 * QB] + p_c[2 * QB:3 * QB] + p_c[3 * QB:4 * QB]
    cur = t // CMP_BLOCK
    imp = jnp.where((n_idx == cur) | (n_idx == 0), FORCED, jnp.where(n_idx > cur, -1.0, imp))
    imp = jnp.where(real, imp, -2.0)
    impT = imp.T[64:128]
    n_row = lax.broadcasted_iota(jnp.int32, (64, QB), 0)
    curT = (j * QB + lax.broadcasted_iota(jnp.int32, (64, QB), 1)) // CMP_BLOCK
    rankT = jnp.zeros((64, QB), jnp.int32)
    for mm in range(nb):
        rowm = impT[mm:mm + 1, :]
        beats = (rowm > impT) | ((rowm == impT) & (n_row > mm))
        rankT = rankT + beats.astype(jnp.int32)
    selT = (rankT < N_SELECT) & (n_row <= curT) & (n_row < nb)
    biasT = jnp.where(selT, 0.0, UNSELECTED)
    bias = jnp.concatenate([jnp.zeros((64, QB), F32), biasT], axis=0).T
    q_sel = (q4.astype(F32) + rep4(bias)).astype(BF16)

    def ones_rows(width):
        return jnp.where(lax.broadcasted_iota(jnp.int32, (64, width), 0) == 0, 1.0, 0.0).astype(BF16)

    def attend(q, k_ref_, v_ref_, off, width, bias_q, carry):
        m, acc = carry
        k_aug = jnp.concatenate([k_ref_[0, :, pl.ds(off, width)], oh_ref[:, pl.ds(off, width)]], axis=0)
        s = _dot(q, k_aug)
        if bias_q is not None:
            s = s + rep4(bias_q)
        m_new = jnp.maximum(m, jnp.max(s, axis=1, keepdims=True))
        p = jnp.exp(s - m_new).astype(BF16)
        v_aug = jnp.concatenate([v_ref_[0, :, pl.ds(off, width)], ones_rows(width)], axis=0)
        acc = jnp.exp(m - m_new) * acc + _nt(p, v_aug)
        return m_new, acc

    init = (jnp.full((R, 1), NEG_INF, F32), jnp.zeros((R, 128), F32))

    last = (j * QB) // ch
    carry = lax.fori_loop(
        0, last, lambda c, cr: attend(q_sel, ks_ref, vs_ref, pl.multiple_of(c * ch, ch), ch, None, cr), init)
    tq_c = j * QB + lax.broadcasted_iota(jnp.int32, (QB, ch), 0)
    kp_c = last * ch + lax.broadcasted_iota(jnp.int32, (QB, ch), 1)
    _, acc_s = attend(q_sel, ks_ref, vs_ref, pl.multiple_of(last * ch, ch), ch,
                      jnp.where(kp_c <= tq_c, 0.0, NEG_INF), carry)
    o_s = acc_s / acc_s[:, 64:65]

    ws = jnp.clip(j - WINDOW // QB, 0, n_tiles - ww // QB)
    dpos = (j - ws) * QB + lax.broadcasted_iota(jnp.int32, (QB, ww), 0) \
        - lax.broadcasted_iota(jnp.int32, (QB, ww), 1)
    _, acc_w = attend(q4, kw_ref, vw_ref, pl.multiple_of(ws * QB, QB), ww,
                      jnp.where((dpos >= 0) & (dpos < WINDOW), 0.0, NEG_INF), init)
    o_w = acc_w / acc_w[:, 64:65]

    ga = ga_ref[0, 0]
    outs = []
    for g in range(NSA_GROUP):
        sl = slice(g * QB, (g + 1) * QB)
        outs.append(ga[:, 3 * g:3 * g + 1] * o_c[sl] + ga[:, 3 * g + 1:3 * g + 2] * o_s[sl]
                    + ga[:, 3 * g + 2:3 * g + 3] * o_w[sl])
    lo = lane < 64
    o_ref[0, :, 0:128] = jnp.where(lo, outs[0], pltpu.roll(outs[1], 64, 1))
    o_ref[0, :, 128:256] = jnp.where(lo, outs[2], pltpu.roll(outs[3], 64, 1))


def _nsa_prompt(qn, kcb, vcb, kvb, onehot, ga):
    B, H, T, _ = qn.shape
    nb = T // CMP_BLOCK
    assert nb <= 64
    QB = 128
    grid = (B, NSA_KV_HEADS, T // QB)
    kvspec = lambda base: pl.BlockSpec((1, 64, T), lambda b, kv, j: (b, base + kv, 0))
    return pl.pallas_call(
        functools.partial(_nsa_kernel, nb=nb, ch=min(NSA_CH, T), ww=min(WINDOW + QB, T), n_tiles=T // QB),
        grid=grid,
        in_specs=[
            pl.BlockSpec((1, NSA_GROUP, QB, 128), lambda b, kv, j: (b, kv, j, 0)),
            pl.BlockSpec((1, 1, 128, 128), lambda b, kv, j: (b, kv, 0, 0)),
            pl.BlockSpec((1, 1, 128, 128), lambda b, kv, j: (b, kv, 0, 0)),
            kvspec(0), kvspec(2), kvspec(4), kvspec(6),
            pl.BlockSpec((64, T), lambda b, kv, j: (0, 0)),
            pl.BlockSpec((1, 1, QB, 128), lambda b, kv, j: (b, kv, j, 0)),
        ],
        out_specs=pl.BlockSpec((1, QB, 256), lambda b, kv, j: (b, j, kv)),
        out_shape=jax.ShapeDtypeStruct((B, T, 512), F32),
        compiler_params=_cparams(3), name="nsa_prompt",
    )(qn, kcb, vcb, kvb, kvb, kvb, kvb, onehot, ga)


def _tail_kernel(x_ref, oa_ref, ob_ref, gm_ref, p_ref, wua_ref, wub_ref, wo_ref, nf_ref,
                 wg_ref, wu_ref, wd_ref, npl_ref, wpg_ref, wpp_ref, y_ref):
    D = x_ref.shape[-1]
    u_a = _dot(oa_ref[...].astype(BF16), wua_ref[...])
    u_b = _dot(ob_ref[...].astype(BF16), wub_ref[...])
    mixed = gm_ref[:, 0:D] * u_a + gm_ref[:, D:2 * D] * u_b
    x1 = x_ref[...] + _dot(mixed.astype(BF16), wo_ref[...])

    def rms(v, g):
        return (v * lax.rsqrt(jnp.mean(v * v, axis=-1, keepdims=True) + EPS) * g).astype(BF16)

    h = rms(x1, nf_ref[...])
    ff = jax.nn.silu(_dot(h, wg_ref[...])) * _dot(h, wu_ref[...])
    x2 = x1 + _dot(ff.astype(BF16), wd_ref[...])
    gate = jax.nn.sigmoid(_dot(rms(x2, npl_ref[...]), wpg_ref[...]))
    y_ref[...] = x2 + gate * _dot(p_ref[...].astype(BF16), wpp_ref[...])


def _tail(x, oa, ob, gm, p, w, tm):
    N, D = x.shape
    tok = lambda width: pl.BlockSpec((tm, width), lambda i: (i, 0))
    const = lambda a: pl.BlockSpec(a.shape, lambda i: (0,) * a.ndim, pipeline_mode=pl.Buffered(1))
    return pl.pallas_call(
        _tail_kernel,
        grid=(N // tm,),
        in_specs=[tok(D), tok(oa.shape[1]), tok(ob.shape[1]), tok(gm.shape[1]), tok(p.shape[1])]
        + [const(a) for a in w],
        out_specs=tok(D),
        out_shape=jax.ShapeDtypeStruct((N, D), F32),
        compiler_params=_cparams(1), name="tail",
    )(x, oa, ob, gm, p, *w)


def _prep_layer(l, norm_mix, w_in, q_norm_nsa, k_norm_nsa, q_norm_fox, k_norm_fox, fox_fbias,
                cmp_pe, w_cmp, w_up_nsa, w_up_fox, w_out, norm_ffn, w_ffn_gate, w_ffn_up,
                w_ffn_down, norm_ple, w_ple_gate, w_ple_proj):
    wl = w_in[l]
    D = wl.shape[0]
    o_ga = 512 + 6 * 128
    o_qb = o_ga + 3 * NSA_HEADS
    o_fb = o_qb + 3 * 512
    o_gm = o_fb + FOX_HEADS
    wt = jnp.concatenate([wl[:, 0:o_ga], wl[:, o_qb:o_gm], jnp.zeros((D, 8), F32)], axis=1).T.astype(BF16)
    zpad = jnp.zeros((D, 128 - 12), F32)
    wg = jnp.concatenate([wl[:, o_ga:o_ga + 12], zpad, wl[:, o_ga + 12:o_ga + 24], zpad, wl[:, o_gm:]],
                         axis=1).astype(BF16)
    hg = jnp.stack([q_norm_nsa[l], k_norm_nsa[l, 0], k_norm_nsa[l, 1], k_norm_nsa[l, 2],
                    q_norm_fox[l], k_norm_fox[l], jnp.zeros((64,), F32), jnp.zeros((64,), F32)], axis=1)
    z64 = jnp.zeros((64, 64), F32)
    return dict(
        gmix=norm_mix[l][None, :], wt=wt, wg=wg, hg=hg, fb=fox_fbias[l][:, None],
        pe=cmp_pe[l], peT=jnp.swapaxes(cmp_pe[l], 1, 2),
        w0T=jnp.concatenate([w_cmp[l, 0].T, z64], axis=0).astype(BF16),
        w1=jnp.concatenate([w_cmp[l, 1], z64], axis=1).astype(BF16),
        w_cmp=w_cmp[l],
        tail=(w_up_nsa[l].astype(BF16), w_up_fox[l].astype(BF16), w_out[l].astype(BF16),
              norm_ffn[l][None, :], w_ffn_gate[l].astype(BF16), w_ffn_up[l].astype(BF16),
              w_ffn_down[l].astype(BF16), norm_ple[l][None, :], w_ple_gate[l].astype(BF16),
              w_ple_proj[l].astype(BF16)),
    )


def _rope_tables(pos):
    half = HEAD_DIM // 2
    inv_freq = ROPE_THETA ** (-jnp.arange(half, dtype=F32) / half)
    ang = inv_freq[:, None] * pos.astype(F32)[None, :]
    return jnp.cos(ang), jnp.sin(ang)


def _prompt_consts(T):
    tpos = jnp.arange(T, dtype=jnp.int32)
    blk = tpos // CMP_BLOCK
    avg = jnp.where(jnp.arange(128)[None, :] == 64 + blk[:, None], 1.0 / CMP_BLOCK, 0.0).astype(BF16)
    onehot = (jnp.arange(64)[:, None] == blk[None, :]).astype(BF16)
    cos, sin = _rope_tables(tpos)
    return dict(avg=avg, avgT=avg.T, onehot=onehot, cos=cos, sin=sin)


def _prompt_layer(xp, p_l, L, C, tm_proj, tq_fox, tm_tail):
    B, T, D = xp.shape
    nsaT, winT, foxT, logfT, qn, qf, kf, kvb, ga, gm, _ = _proj(
        xp, L["gmix"], L["wt"], L["wg"], C["cos"], C["sin"], L["hg"], L["fb"], tm_proj)
    kcb, vcb = _cmp_prompt(nsaT, C["avg"], C["avgT"], L["peT"], L["pe"], L["w0T"], L["w1"])
    o_a = _nsa_prompt(qn, kcb, vcb, kvb, C["onehot"], ga)
    o_b = _fox_prompt(qf, kf, kvb, tq_fox)
    N = B * T
    y = _tail(xp.reshape(N, D), o_a.reshape(N, 512), o_b.reshape(N, 512), gm.reshape(N, gm.shape[-1]),
              p_l.reshape(N, p_l.shape[-1]), L["tail"], tm_tail).reshape(B, T, D)
    nsa_rows = jnp.transpose(nsaT.reshape(B, 4, NSA_KV_HEADS, HEAD_DIM, T), (0, 4, 1, 2, 3))
    fox_rows = jnp.transpose(foxT.reshape(B, 2, FOX_HEADS, HEAD_DIM, T), (0, 4, 1, 2, 3))
    logf = jnp.transpose(logfT, (0, 2, 1))
    wk = min(WINDOW, T)
    win_rows = jnp.transpose(winT[:, :, T - wk:].reshape(B, 2, NSA_KV_HEADS, HEAD_DIM, wk), (0, 4, 1, 2, 3))
    return y, nsa_rows, fox_rows, logf, win_rows


def _lane_sum_rows(a):
    ones = jnp.ones((8, a.shape[1]), BF16)
    a1, a2, a3 = _split3(a)
    return _nt(ones, a1) + _nt(ones, a2) + _nt(ones, a3)


def _split2(x):
    a = x.astype(BF16)
    return a, (x - a.astype(F32)).astype(BF16)


_RING = 16


def _fox_decode_kernel(pt_ref, cf_ref, clf_ref, qb_ref, kn_ref, vn_ref, lfn_ref, o_ref,
                       ring, s_ref, lf_ref, w_ref, t_ref, acc_ref, sn_ref, sem, lsem, *, layer, n_pages):
    b = pl.program_id(0)
    n_seq = pl.num_programs(0)
    NP = n_pages
    H = FOX_HEADS

    def page_copy(seq, i, slot):
        c = i // NP
        pg = i - c * NP
        return pltpu.make_async_copy(cf_ref.at[layer, pt_ref[seq, pg], c], ring.at[slot], sem.at[slot])

    def logf_copy(pg):
        return pltpu.make_async_copy(clf_ref.at[layer, pt_ref[b, pg]], lf_ref.at[pg], lsem)

    def start_logf(pg, c):
        logf_copy(pg).start()
        return c

    lax.fori_loop(0, NP, start_logf, 0)

    @pl.when(b == 0)
    def _():
        for s in range(_RING - 2):
            page_copy(b, s, s).start()

    def advance_pair(i):
        slots = [(i + u) % _RING for u in range(2)]
        for u in range(2):
            page_copy(b, i + u, slots[u]).wait()
        for u in range(2):
            nxt = i + u + _RING - 2

            @pl.when(nxt < 2 * NP)
            def _():
                page_copy(b, nxt, nxt % _RING).start()

            @pl.when((nxt >= 2 * NP) & (b + 1 < n_seq))
            def _():
                page_copy(b + 1, nxt - 2 * NP, nxt % _RING).start()

        return slots

    def kbody(it, c):
        i = it * 2
        slots = advance_pair(i)
        for h in range(H):
            qh = qb_ref[0, h]
            for u in range(2):
                s_ref[i + u, h:h + 1, :] = jnp.sum(ring[slots[u], h] * qh, axis=0, keepdims=True)
        return c

    lax.fori_loop(0, NP // 2, kbody, 0)

    def wait_logf(pg, c):
        logf_copy(pg).wait()
        return c

    lax.fori_loop(0, NP, wait_logf, 0)

    ii = lax.broadcasted_iota(jnp.int32, (128, 128), 0)
    jj = lax.broadcasted_iota(jnp.int32, (128, 128), 1)
    later = jnp.where(ii > jj, 1.0, 0.0).astype(BF16)
    ones = jnp.ones((128, 128), BF16)
    l1, l2, l3 = _split3(lf_ref[...].reshape(NP * 8, 128))
    w_ref[...] = (_dot(l1, later) + _dot(l2, later) + _dot(l3, later)).reshape(NP, 8, 128)
    t_ref[...] = (_dot(l1, ones) + _dot(l2, ones) + _dot(l3, ones)).reshape(NP, 8, 128)
    for h in range(H):
        sn_ref[h:h + 1, :] = jnp.sum(kn_ref[0, h] * qb_ref[0, h], axis=0, keepdims=True)
    cn = lfn_ref[0]
    s_new = sn_ref[...] + cn - cn

    def sbody(k, carry):
        tot, mx = carry
        pg = NP - 1 - k
        sc = s_ref[pg] + cn + w_ref[pg] + tot
        s_ref[pg] = sc
        return tot + t_ref[pg], jnp.maximum(mx, sc)

    _, mx = lax.fori_loop(0, NP, sbody, (jnp.zeros((H, 128), F32), jnp.full((H, 128), NEG_INF, F32)))
    m = jnp.maximum(jnp.max(mx, axis=1, keepdims=True), s_new)
    p = jnp.exp(s_ref[...] - m[None])
    s_ref[...] = p
    p_new = jnp.exp(s_new - m)
    denom = jnp.sum(jnp.sum(p, axis=0), axis=1, keepdims=True) + p_new[:, 0:1]

    acc_ref[...] = jnp.zeros_like(acc_ref)

    def vbody(it, c):
        i = NP + it * 2
        slots = advance_pair(i)
        for h in range(H):
            a = acc_ref[h]
            for u in range(2):
                a = a + ring[slots[u], h] * s_ref[i - NP + u, h:h + 1, :]
            acc_ref[h] = a
        return c

    lax.fori_loop(0, NP // 2, vbody, 0)

    lane = lax.broadcasted_iota(jnp.int32, (HEAD_DIM, 128), 1)
    for h in range(H):
        a = acc_ref[h] + jnp.where(lane == 0, p_new[h:h + 1, :] * vn_ref[0, h], 0.0)
        o_ref[0, h:h + 1, :] = _lane_sum_rows(a)[0:1] / denom[h:h + 1]


def _fox_decode(layer, page_table, cfT, clfT, qb, knb, vnb, lfnb):
    DB, NP = page_table.shape
    assert NP % 2 == 0 and (2 * NP) % _RING == 0
    per_seq = lambda *shape: pl.BlockSpec((1,) + shape, lambda b, pt: (b,) + (0,) * len(shape))
    page_f32 = pltpu.VMEM((NP, 8, 128), F32)
    grid_spec = pltpu.PrefetchScalarGridSpec(
        num_scalar_prefetch=1, grid=(DB,),
        in_specs=[pl.BlockSpec(memory_space=pl.ANY), pl.BlockSpec(memory_space=pl.ANY),
                  per_seq(8, 64, 128), per_seq(8, 64, 128), per_seq(8, 64, 128), per_seq(8, 128)],
        out_specs=per_seq(8, 64),
        scratch_shapes=[pltpu.VMEM((_RING, 8, 64, 128), F32), page_f32, page_f32, page_f32, page_f32,
                        pltpu.VMEM((8, 64, 128), F32), pltpu.VMEM((8, 128), F32),
                        pltpu.SemaphoreType.DMA((_RING,)), pltpu.SemaphoreType.DMA(())])
    return pl.pallas_call(
        functools.partial(_fox_decode_kernel, layer=layer, n_pages=NP),
        grid_spec=grid_spec, out_shape=jax.ShapeDtypeStruct((DB, 8, 64), F32),
        compiler_params=_cparams(1), name="fox_sample",
    )(page_table, cfT, clfT, qb, knb, vnb, lfnb)


def _nsa_decode_kernel(pt_ref, cn_ref, qb_ref, qrow_ref, newb_ref, newrow_ref, vnr_ref, swin_ref,
                       gate_ref, avg8_ref, wbk_ref, wbv_ref, pem_ref, exp_ref,
                       o_ref, nwin_ref,
                       ring, s_ref, m_ref, acc_ref, sn_ref, sem, *, layer, n_pages):
    b = pl.program_id(0)
    NP = n_pages
    G = NP // 4
    P = NP * PAGE_SIZE
    nbp = P // CMP_BLOCK
    NBP = m_ref.shape[0]
    H = NSA_HEADS
    wb = swin_ref.shape[-1]
    kch = exp_ref.shape[1]

    n_seq = pl.num_programs(0)
    n_grp = _RING // 4
    ahead = n_grp

    def copy1(seq, pg, slot):
        return pltpu.make_async_copy(cn_ref.at[layer, pt_ref[seq, pg], pl.ds(0, 3)], ring.at[slot], sem.at[slot])

    def copy2(seq, pg, slot):
        return pltpu.make_async_copy(cn_ref.at[layer, pt_ref[seq, pg], 3], ring.at[slot, 0], sem.at[slot])

    def start_group(seq, gi):
        base = (gi % n_grp) * 4

        @pl.when(gi < G)
        def _():
            for u in range(4):
                copy1(seq, gi * 4 + u, base + u).start()

        @pl.when(gi >= G)
        def _():
            for u in range(4):
                copy2(seq, (gi - G) * 4 + u, base + u).start()

    def start_ahead(g):
        nxt = g + ahead

        @pl.when(nxt < 2 * G)
        def _():
            start_group(b, nxt)

        @pl.when((nxt >= 2 * G) & (b + 1 < n_seq))
        def _():
            start_group(b + 1, nxt - 2 * G)

    @pl.when(b == 0)
    def _():
        for gi in range(ahead):
            start_group(b, jnp.int32(gi))

    m_ref[...] = jnp.zeros_like(m_ref)
    zk64 = jnp.zeros((HEAD_DIM, 128), BF16)

    def body1(g, c):
        base = (g % n_grp) * 4
        for u in range(4):
            copy1(b, g * 4 + u, base + u).wait()
        contrib = jnp.zeros((8, 256), F32)
        for u in range(4):
            x1, x2 = _split2(ring[base + u, 0:2].reshape(256, 128))
            contrib = contrib + _nt(avg8_ref[u], x1) + _nt(avg8_ref[u], x2)
        m_ref[pl.ds(pl.multiple_of(g * 8, 8), 8), :] = contrib
        for u in range(4):
            off = pl.multiple_of(g * 512 + u * 128, 128)
            ks = ring[base + u, 2].astype(BF16)
            s_ref[:, pl.ds(off, 128)] = (_dot(qrow_ref[0, 0], jnp.concatenate([ks[0], zk64], axis=0))
                                         + _dot(qrow_ref[0, 1], jnp.concatenate([ks[1], zk64], axis=0)))
        start_ahead(g)
        return c

    lax.fori_loop(0, G, body1, 0)

    rowm = lax.broadcasted_iota(jnp.int32, (NBP, 1), 0)
    pem = pem_ref[...]
    mm = m_ref[...] + jnp.where(rowm < nbp, pem, 0.0)
    mm = jnp.where(rowm == nbp, newrow_ref[0, 0:1, :] * (1.0 / CMP_BLOCK) + pem, mm)
    mb = mm.astype(BF16)
    n8 = lax.broadcasted_iota(jnp.int32, (8, NBP), 1)
    row8 = lax.broadcasted_iota(jnp.int32, (8, NBP), 0)
    rowo = lax.broadcasted_iota(jnp.int32, (8, 128), 0)
    cur = nbp
    kcb = [_dot(mb, wbk_ref[kv]).astype(BF16) for kv in range(NSA_KV_HEADS)]
    vcb = [_dot(mb, wbv_ref[kv]).astype(BF16) for kv in range(NSA_KV_HEADS)]
    sc = _nt(qrow_ref[0, 0], kcb[0]) + _nt(qrow_ref[0, 1], kcb[1])
    valid = (n8 + 1) * CMP_BLOCK - 1 <= P
    s = jnp.where(valid, sc, NEG_INF)
    e = jnp.exp(s - jnp.max(s, axis=1, keepdims=True))
    p_c = jnp.where(valid, e / jnp.sum(e, axis=1, keepdims=True), 0.0)
    o_c = (_dot(jnp.where(row8 < NSA_GROUP, p_c, 0.0).astype(BF16), vcb[0])
           + _dot(jnp.where(row8 >= NSA_GROUP, p_c, 0.0).astype(BF16), vcb[1]))
    imp0 = p_c[0:1] + p_c[1:2] + p_c[2:3] + p_c[3:4]
    imp1 = p_c[4:5] + p_c[5:6] + p_c[6:7] + p_c[7:8]
    n1 = n8[0:1]
    rowp = lax.broadcasted_iota(jnp.int32, (128, NBP), 0)
    m_idx = lax.broadcasted_iota(jnp.int32, (NBP, NBP), 0)
    n_idx = lax.broadcasted_iota(jnp.int32, (NBP, NBP), 1)
    imps = []
    for imp in (imp0, imp1):
        imp = jnp.where((n1 == cur) | (n1 == 0), FORCED, imp)
        imps.append(jnp.where(n1 > cur, -2.0, imp))
    impT = jnp.where(rowp == 0, imps[0], jnp.where(rowp == 1, imps[1], 0.0)).T
    sels = []
    for kv in range(NSA_KV_HEADS):
        col = impT[:, kv:kv + 1]
        beats = (col > imps[kv]) | ((col == imps[kv]) & (m_idx < n_idx))
        rank = jnp.sum(jnp.where(beats, 1.0, 0.0), axis=0, keepdims=True)
        sels.append(jnp.where((rank < N_SELECT) & (n1 <= cur), 1.0, 0.0))
    sel8 = jnp.where(row8 < NSA_GROUP, sels[0], sels[1]).astype(BF16)

    mx = jnp.full((H, 1), NEG_INF, F32)
    for ci in range(P // kch):
        mk = _dot(sel8[:, ci * 128:(ci + 1) * 128], exp_ref[...])
        sm = jnp.where(mk > 0.5, s_ref[:, ci * kch:(ci + 1) * kch], NEG_INF)
        s_ref[:, ci * kch:(ci + 1) * kch] = sm
        mx = jnp.maximum(mx, jnp.max(sm, axis=1, keepdims=True))
    for h in range(H):
        kv = h // NSA_GROUP
        sn_ref[h:h + 1, :] = jnp.sum(newb_ref[0, 256 + kv * 64:256 + (kv + 1) * 64, :] * qb_ref[0, h],
                                     axis=0, keepdims=True)
    s_new = sn_ref[...]
    m = jnp.maximum(mx, s_new)
    lsum = jnp.zeros((H, 1), F32)
    for ci in range(P // kch):
        p = jnp.exp(s_ref[:, ci * kch:(ci + 1) * kch] - m[:, 0:1])
        s_ref[:, ci * kch:(ci + 1) * kch] = p
        lsum = lsum + jnp.sum(p, axis=1, keepdims=True)
    p_new = jnp.exp(s_new - m)
    denom = lsum + p_new[:, 0:1]

    acc_ref[...] = jnp.zeros_like(acc_ref)

    def body2(g, c):
        base = (g % n_grp) * 4
        for u in range(4):
            copy2(b, (g - G) * 4 + u, base + u).wait()
        for h in range(H):
            a = acc_ref[h]
            for u in range(4):
                off = pl.multiple_of((g - G) * 512 + u * 128, 128)
                a = a + ring[base + u, 0, h // NSA_GROUP] * s_ref[h:h + 1, pl.ds(off, 128)]
            acc_ref[h] = a
        start_ahead(g)
        return c

    lax.fori_loop(G, 2 * G, body2, 0)

    zpad = jnp.zeros((HEAD_DIM, 128), F32)
    o_s = jnp.zeros((8, 128), F32)
    for h in range(H):
        r = _lane_sum_rows(jnp.concatenate([acc_ref[h], zpad], axis=0))
        o_s = jnp.where(rowo == h, r, o_s)
    o_s = (o_s + p_new[:, 0:1] * vnr_ref[0, 0]) / denom

    lane5 = lax.broadcasted_iota(jnp.int32, (8, wb), 1)
    row5 = lax.broadcasted_iota(jnp.int32, (8, wb), 0)
    zkw = jnp.zeros((HEAD_DIM, wb), BF16)
    sw = jnp.zeros((8, wb), F32)
    for kv in range(NSA_KV_HEADS):
        sw = sw + _dot(qrow_ref[0, kv], jnp.concatenate([swin_ref[0, 0, 0, kv].astype(BF16), zkw], axis=0))
    valid_w = (wb - lane5) < WINDOW
    sw = jnp.where(valid_w, sw, NEG_INF)
    for h in range(H):
        kv = h // NSA_GROUP
        sn_ref[h:h + 1, :] = jnp.sum(newb_ref[0, 512 + kv * 64:512 + (kv + 1) * 64, :] * qb_ref[0, h],
                                     axis=0, keepdims=True)
    s_wn = sn_ref[:, 0:1]
    m_w = jnp.maximum(jnp.max(sw, axis=1, keepdims=True), s_wn)
    p_w = jnp.where(valid_w, jnp.exp(sw - m_w), 0.0)
    p_wn = jnp.exp(s_wn - m_w)
    l_w = jnp.sum(p_w, axis=1, keepdims=True) + p_wn
    o_w = p_wn * vnr_ref[0, 1]
    for kv in range(NSA_KV_HEADS):
        mine = (row5 < NSA_GROUP) if kv == 0 else (row5 >= NSA_GROUP)
        o_w = o_w + _nt(jnp.where(mine, p_w, 0.0).astype(BF16),
                        jnp.concatenate([swin_ref[0, 0, 1, kv].astype(BF16), zkw], axis=0))
    o_w = o_w / l_w
    o_ref[0] = gate_ref[0, 0] * o_c + gate_ref[0, 1] * o_s + gate_ref[0, 2] * o_w

    lanew = lax.broadcasted_iota(jnp.int32, (HEAD_DIM, wb), 1)
    for kv in range(NSA_KV_HEADS):
        for c in range(2):
            col = newb_ref[0, 512 + c * 128 + kv * 64:512 + c * 128 + (kv + 1) * 64, 0:1]
            rolled = pltpu.roll(swin_ref[0, 0, c, kv], wb - 1, 1)
            nwin_ref[0, c, kv] = jnp.where(lanew == wb - 1, col, rolled)


def _nsa_decode(layer, page_table, cnT, qb, qrow, newb, newrow, vnr, swT, gates, CS):
    DB, NP = page_table.shape
    P = NP * PAGE_SIZE
    wb = swT.shape[-1]
    NBP = -(-(P // CMP_BLOCK + 1) // 128) * 128
    assert NP % 4 == 0 and (2 * NP) % _RING == 0
    per_seq = lambda *shape: pl.BlockSpec((1,) + shape, lambda b, pt: (b,) + (0,) * len(shape))
    const = lambda a: pl.BlockSpec(a.shape, lambda b, pt: (0,) * a.ndim)
    consts = [CS["avg8"], CS["wbk"], CS["wbv"], CS["pem"], CS["expand"]]
    grid_spec = pltpu.PrefetchScalarGridSpec(
        num_scalar_prefetch=1, grid=(DB,),
        in_specs=[pl.BlockSpec(memory_space=pl.ANY),
                  per_seq(8, 64, 128), per_seq(2, 8, 128), per_seq(768, 128), per_seq(8, 256),
                  per_seq(2, 8, 128),
                  pl.BlockSpec((1, 1, 2, 2, 64, wb), lambda b, pt: (layer, b, 0, 0, 0, 0)),
                  per_seq(3, 8, 128)] + [const(a) for a in consts],
        out_specs=[per_seq(8, 128), per_seq(2, 2, 64, wb)],
        scratch_shapes=[pltpu.VMEM((_RING, 3, 2, 64, 128), F32), pltpu.VMEM((8, P), F32),
                        pltpu.VMEM((NBP, 256), F32), pltpu.VMEM((8, 64, 128), F32),
                        pltpu.VMEM((8, 128), F32), pltpu.SemaphoreType.DMA((_RING,))])
    return pl.pallas_call(
        functools.partial(_nsa_decode_kernel, layer=layer, n_pages=NP),
        grid_spec=grid_spec,
        out_shape=[jax.ShapeDtypeStruct((DB, 8, 128), F32),
                   jax.ShapeDtypeStruct((DB, 2, 2, 64, wb), F32)],
        compiler_params=_cparams(1), name="nsa_sample",
    )(page_table, cnT, qb, qrow, newb, newrow, vnr, swT, gates, *consts)


def _sample_consts(P, L):
    r = jnp.arange(128)
    avg8 = jnp.stack([jnp.where(jnp.arange(8)[:, None] == 2 * p + r[None, :] // CMP_BLOCK,
                                1.0 / CMP_BLOCK, 0.0) for p in range(4)]).astype(BF16)
    w = L["w_cmp"]
    z = jnp.zeros((64, 128), F32)
    wpad = lambda c: jnp.concatenate([w[c], jnp.zeros((64, 64), F32)], axis=1)
    blocks = lambda c, kv: jnp.concatenate(
        [wpad(c) if (cc, kk) == (c, kv) else z for cc in range(2) for kk in range(2)], axis=0)
    wbk = jnp.stack([blocks(0, kv) for kv in range(2)]).astype(BF16)
    wbv = jnp.stack([blocks(1, kv) for kv in range(2)]).astype(BF16)
    pm = jnp.mean(L["pe"], axis=1)
    pem = jnp.concatenate([pm[0], pm[0], pm[1], pm[1]])[None, :]
    kch = min(P, 8192)
    expand = (r[:, None] == (jnp.arange(kch) // CMP_BLOCK)[None, :]).astype(BF16)
    return dict(avg8=avg8, wbk=wbk, wbv=wbv, pem=pem, expand=expand)


def _sample_layer(layer, xs, p_l, L, cos_s, sin_s, page_table, cnT, cfT, clfT, swT):
    DB, D = xs.shape
    P = page_table.shape[1] * PAGE_SIZE
    xpad = jnp.zeros((1, 128, D), F32).at[0, :DB].set(xs)
    nsaT, winT, foxT, logfT, qn, qf, kf, kvb, ga, gm, qT = _proj(
        xpad, L["gmix"], L["wt"], L["wg"], cos_s, sin_s, L["hg"], L["fb"], 128)
    lanes = lambda a: jnp.broadcast_to(a[..., None], a.shape + (128,))
    q_all = (qT[0, :, :DB] * SCALE).T
    qb_a = lanes(q_all[:, :512].reshape(DB, 8, 64))
    qb_f = lanes(q_all[:, 512:].reshape(DB, 8, 64))
    new = jnp.concatenate([nsaT[0], winT[0]], axis=0)[:, :DB].T
    fx = foxT[0][:, :DB].T
    lf = logfT[0][:, :DB].T
    q8 = jnp.transpose(qn[0, :, :DB, :], (1, 0, 2))
    zq = jnp.zeros_like(q8[:, :4])
    qrow = jnp.stack([jnp.concatenate([q8[:, :4], zq], axis=1),
                      jnp.concatenate([zq, q8[:, 4:]], axis=1)], axis=1)
    g = jnp.transpose(ga[0, :, :DB, :12].reshape(2, DB, 4, 3), (1, 3, 0, 2))
    gates = lanes(g.reshape(DB, 3, 8))
    vrow = jnp.stack([new[:, 384:512].reshape(DB, 2, 64), new[:, 640:768].reshape(DB, 2, 64)], axis=1)
    vrow = jnp.repeat(vrow, NSA_GROUP, axis=2)
    vnr = jnp.concatenate([vrow, jnp.zeros_like(vrow)], axis=-1)
    newrow = jnp.broadcast_to(new[:, None, :256], (DB, 8, 256))
    CS = _sample_consts(P, L)
    o_a8, new_win = _nsa_decode(layer, page_table, cnT, qb_a, qrow, lanes(new), newrow, vnr, swT, gates, CS)
    o_a = o_a8[:, :, :64].reshape(DB, 512)
    o_b = _fox_decode(layer, page_table, cfT, clfT, qb_f, lanes(fx[:, :512].reshape(DB, 8, 64)),
                      lanes(fx[:, 512:].reshape(DB, 8, 64)), lanes(lf)).reshape(DB, 512)
    y = _tail(xs, o_a, o_b, gm[0, :DB], p_l, L["tail"], DB)
    return (y, new[:, :512].reshape(DB, 1, 4, NSA_KV_HEADS, HEAD_DIM),
            fx.reshape(DB, 1, 2, FOX_HEADS, HEAD_DIM), lf.reshape(DB, 1, FOX_HEADS),
            jnp.transpose(new_win, (0, 4, 1, 2, 3)))


def kernel(x_prompt, x_sample, p_prompt, p_sample, cache_nsa, cache_fox, cache_fox_logf, state_win, page_table, norm_mix, w_in, q_norm_nsa, k_norm_nsa, q_norm_fox, k_norm_fox, fox_fbias, cmp_pe, w_cmp, w_up_nsa, w_up_fox, w_out, norm_ffn, w_ffn_gate, w_ffn_up, w_ffn_down, norm_ple, w_ple_gate, w_ple_proj):
    depth = w_in.shape[0]
    B, T, D = x_prompt.shape
    DB, S, _ = x_sample.shape
    assert S == 1 and DB <= 128
    P = page_table.shape[1] * PAGE_SIZE
    cnT = jnp.transpose(cache_nsa, (0, 1, 3, 4, 5, 2))
    cfT = jnp.transpose(cache_fox, (0, 1, 3, 4, 5, 2))
    clfT = jnp.transpose(cache_fox_logf, (0, 1, 3, 2))
    swT = jnp.transpose(state_win, (0, 1, 3, 4, 5, 2))
    C = _prompt_consts(T)
    cos_s, sin_s = _rope_tables(jnp.full((128,), P, jnp.int32))
    tm, tq = min(PROJ_TM, T), min(FOX_TQ, T)
    xp, xs = x_prompt, x_sample.reshape(DB, D)
    outs = [[] for _ in range(8)]
    for l in range(depth):
        L = _prep_layer(l, norm_mix, w_in, q_norm_nsa, k_norm_nsa, q_norm_fox, k_norm_fox, fox_fbias,
                        cmp_pe, w_cmp, w_up_nsa, w_up_fox, w_out, norm_ffn, w_ffn_gate, w_ffn_up,
                        w_ffn_down, norm_ple, w_ple_gate, w_ple_proj)
        xp, nsa_p, fox_p, lf_p, win_p = _prompt_layer(xp, p_prompt[l], L, C, tm, tq, min(TAIL_TM, T))
        xs, nsa_s, fox_s, lf_s, win_s = _sample_layer(l, xs, p_sample[l].reshape(DB, -1), L, cos_s, sin_s,
                                                      page_table, cnT, cfT, clfT, swT)
        for acc, v in zip(outs, (nsa_p, nsa_s, fox_p, fox_s, lf_p, lf_s, win_p, win_s)):
            acc.append(v)
    return (xp, xs.reshape(DB, 1, D)) + tuple(jnp.stack(o) for o in outs)
```
